```python
import math
import jax, jax.numpy as jnp
from jax import lax
import numpy as np

D_MODEL = 1024
BATCH = 16
SEQ = 2048
DEPTH = 2
DEC_BATCH = 16
DEC_SEQ = 64
PAST_LEN = 4096

CHUNK = 64
GDN_HEADS = 4
GDN_DK = 128
GDN_DV = 128
GDN_CONV = 4
GDN_QKV = GDN_HEADS * (2 * GDN_DK + GDN_DV)
GDN_BLOCK = CHUNK
MLP_HEADS = 4
MLP_HEAD_DIM = 64
MLP_WIDTH = MLP_HEADS * MLP_HEAD_DIM
MLP_CHUNK = 128
SWA_HEADS = 4
SWA_KV_HEADS = 2
SWA_GROUP = SWA_HEADS // SWA_KV_HEADS
SWA_HEAD_DIM = 64
WINDOW = 128
SWA_BACK = WINDOW // CHUNK
SWA_SCALE = SWA_HEAD_DIM ** -0.5
MIX_WIDTH = GDN_HEADS * GDN_DV + MLP_WIDTH + SWA_HEADS * SWA_HEAD_DIM
D_FF = -(-8 * D_MODEL // (3 * 256)) * 256
IN_SIZES = (GDN_QKV, GDN_HEADS * GDN_DV, GDN_HEADS, GDN_HEADS, MLP_WIDTH, MLP_WIDTH,
            SWA_HEADS * SWA_HEAD_DIM, SWA_KV_HEADS * SWA_HEAD_DIM, SWA_KV_HEADS * SWA_HEAD_DIM)
IN_COLS = sum(IN_SIZES)
EPS = 1e-6

kernel_name = 'hybrid_gdn_gmlp_swa_stream_step'


def _rms_norm(x, g):
    xf = x.astype(jnp.float32)
    y = xf * lax.rsqrt(jnp.mean(xf * xf, axis=-1, keepdims=True) + EPS)
    return (y * g.astype(jnp.float32)).astype(x.dtype)


def _layer_norm(x, g, b):
    xf = x.astype(jnp.float32)
    mu = jnp.mean(xf, axis=-1, keepdims=True)
    xc = xf - mu
    y = xc * lax.rsqrt(jnp.mean(xc * xc, axis=-1, keepdims=True) + EPS)
    return (y * g.astype(jnp.float32) + b.astype(jnp.float32)).astype(x.dtype)


def _l2norm(x):
    xf = x.astype(jnp.float32)
    return xf * lax.rsqrt(jnp.sum(xf * xf, axis=-1, keepdims=True) + EPS)


def _causal_dwconv(x_ext, w):
    return lax.conv_general_dilated(x_ext, w[:, None, :], window_strides=(1,), padding='VALID',
                                    dimension_numbers=('NWC', 'WIO', 'NWC'),
                                    feature_group_count=x_ext.shape[-1])


def _gated_delta_blocks(q, k, v, g, beta, s0, block):
    bsz, L, H, DK = q.shape
    DV = v.shape[-1]
    n = L // block

    def to_blocks(t):
        t = t.reshape((bsz, n, block, H) + t.shape[3:])
        return jnp.moveaxis(t, (1, 3), (0, 2))

    qb, kb, vb, gb, bb = to_blocks(q), to_blocks(k), to_blocks(v), to_blocks(g), to_blocks(beta)
    gc = jnp.cumsum(gb, axis=-1)
    tri_incl = jnp.tril(jnp.ones((block, block), bool))
    tri_strict = jnp.tril(jnp.ones((block, block), bool), -1)
    decay = jnp.exp(jnp.where(tri_incl, gc[..., :, None] - gc[..., None, :], -jnp.inf))
    kbeta = kb * bb[..., None]
    a_low = jnp.where(tri_strict, jnp.einsum('nbhid,nbhjd->nbhij', kbeta, kb) * decay, 0.0)
    rhs = jnp.concatenate([vb * bb[..., None], kbeta * jnp.exp(gc)[..., None]], axis=-1)
    sol = lax.linalg.triangular_solve(a_low + jnp.eye(block, dtype=a_low.dtype), rhs,
                                      left_side=True, lower=True, unit_diagonal=True)
    u_b, w_b = sol[..., :DV], sol[..., DV:]
    attn_intra = jnp.where(tri_incl, jnp.einsum('nbhid,nbhjd->nbhij', qb, kb) * decay, 0.0)

    def step(S, xs):
        q_i, k_i, u_i, w_i, gc_i, a_i = xs
        v_new = u_i - jnp.einsum('bhcd,bhde->bhce', w_i, S)
        o = (jnp.einsum('bhcd,bhde->bhce', q_i * jnp.exp(gc_i)[..., None], S)
             + jnp.einsum('bhij,bhje->bhie', a_i, v_new))
        g_last = gc_i[..., -1]
        S = (S * jnp.exp(g_last)[..., None, None]
             + jnp.einsum('bhcd,bhce->bhde', k_i * jnp.exp(g_last[..., None] - gc_i)[..., None], v_new))
        return S, o

    S, o = lax.scan(step, s0, (qb, kb, u_b, w_b, gc, attn_intra))
    o = jnp.moveaxis(o, (0, 2), (1, 3)).reshape(bsz, L, H, DV)
    return o, S


def _gdn_mixer(qkv_raw, z, b_raw, a_raw, conv_prev, s0, conv_w, a_log, dt_bias, norm_g, block):
    bsz, L, _ = qkv_raw.shape
    ext = jnp.concatenate([conv_prev.astype(qkv_raw.dtype), qkv_raw], axis=1)
    qkv = jax.nn.silu(_causal_dwconv(ext, conv_w.astype(ext.dtype)))
    q, k, v = jnp.split(qkv, [GDN_HEADS * GDN_DK, 2 * GDN_HEADS * GDN_DK], axis=-1)
    q = _l2norm(q.reshape(bsz, L, GDN_HEADS, GDN_DK)) * (GDN_DK ** -0.5)
    k = _l2norm(k.reshape(bsz, L, GDN_HEADS, GDN_DK))
    v = v.reshape(bsz, L, GDN_HEADS, GDN_DV).astype(jnp.float32)
    beta = jax.nn.sigmoid(b_raw.astype(jnp.float32))
    g = -jnp.exp(a_log.astype(jnp.float32)) * jax.nn.softplus(a_raw.astype(jnp.float32) + dt_bias.astype(jnp.float32))
    o, s_new = _gated_delta_blocks(q, k, v, g, beta, s0.astype(jnp.float32), block)
    gate = jax.nn.silu(z.astype(jnp.float32)).reshape(bsz, L, GDN_HEADS, GDN_DV)
    o = _rms_norm(o, norm_g) * gate
    return o.reshape(bsz, L, GDN_HEADS * GDN_DV).astype(qkv_raw.dtype), s_new, ext[:, -(GDN_CONV - 1):]


def _gmlp_mixer(u_raw, v_raw, ln_g, ln_b, ws, bs, first):
    bsz, L, _ = u_raw.shape
    u = jax.nn.gelu(u_raw).reshape(bsz, L, MLP_HEADS, MLP_HEAD_DIM)
    v = _layer_norm(jax.nn.gelu(v_raw).reshape(bsz, L, MLP_HEADS, MLP_HEAD_DIM), ln_g, ln_b)
    cpos = jnp.arange(MLP_CHUNK) // CHUNK
    ws_m = jnp.where(cpos[:, None] >= cpos[None, :], ws, 0)
    if first:
        n = L // MLP_CHUNK
        vb = v.reshape(bsz, n, MLP_CHUNK, MLP_HEADS, MLP_HEAD_DIM)
        s = jnp.einsum('hij,bnjhd->bnihd', ws_m, vb) + bs.T[:, :, None]
        s = s.reshape(bsz, L, MLP_HEADS, MLP_HEAD_DIM)
    else:
        s = jnp.einsum('hij,bjhd->bihd', ws_m[:, :L, :L], v) + bs[:, :L].T[:, :, None]
    return (u * s).reshape(bsz, L, MLP_WIDTH), v


def _sink_softmax(s, sinks):
    sk = jnp.broadcast_to(sinks.astype(jnp.float32).reshape(SWA_KV_HEADS, SWA_GROUP, 1, 1), s.shape[:-1] + (1,))
    return jax.nn.softmax(jnp.concatenate([s, sk], axis=-1), axis=-1)[..., :-1]


def _swa_prompt(q, k, v, sinks):
    bsz, L = q.shape[:2]
    n = L // CHUNK
    nk = (SWA_BACK + 1) * CHUNK
    qb = q.reshape(bsz, n, CHUNK, SWA_KV_HEADS, SWA_GROUP, SWA_HEAD_DIM)

    def band(t):
        pad = jnp.zeros((bsz, SWA_BACK * CHUNK) + t.shape[2:], t.dtype)
        tp = jnp.concatenate([pad, t], axis=1).reshape(bsz, n + SWA_BACK, CHUNK, SWA_KV_HEADS, SWA_HEAD_DIM)
        return jnp.concatenate([tp[:, j:j + n] for j in range(SWA_BACK + 1)], axis=2)

    kb, vb = band(k), band(v)
    kpos = (jnp.arange(n)[:, None] - SWA_BACK) * CHUNK + jnp.arange(nk)[None, :]
    s = jnp.einsum('bnqkgd,bnskd->bnkgqs', qb, kb).astype(jnp.float32) * SWA_SCALE
    s = jnp.where((kpos >= 0)[None, :, None, None, None, :], s, -jnp.inf)
    p = _sink_softmax(s, sinks).astype(v.dtype)
    o = jnp.einsum('bnkgqs,bnskd->bnqkgd', p, vb)
    return o.reshape(bsz, L, SWA_HEADS * SWA_HEAD_DIM)


def _swa_sample(q, k, v, k_prev, v_prev, sinks):
    bsz, T = q.shape[:2]
    k_all = jnp.concatenate([k_prev.astype(k.dtype), k], axis=1)
    v_all = jnp.concatenate([v_prev.astype(v.dtype), v], axis=1)
    qg = q.reshape(bsz, T, SWA_KV_HEADS, SWA_GROUP, SWA_HEAD_DIM)
    s = jnp.einsum('btkgd,bskd->bkgts', qg, k_all).astype(jnp.float32) * SWA_SCALE
    p = _sink_softmax(s, sinks).astype(v.dtype)
    o = jnp.einsum('bkgts,bskd->btkgd', p, v_all).reshape(bsz, T, SWA_HEADS * SWA_HEAD_DIM)
    return o, k_all, v_all


def _layer(x, first, conv_prev, s0, k_prev, v_prev, norm1_g, w_in, conv_w, a_log, dt_bias, gdn_norm_g,
           ln_g, ln_b, ws, bs, q_norm_g, k_norm_g, sinks, w_out, norm2_g, w_gate, w_up, w_down):
    bsz, L, _ = x.shape
    h = _rms_norm(x, norm1_g)
    parts = jnp.split(h @ w_in, np.cumsum(IN_SIZES)[:-1].tolist(), axis=-1)
    qkv_raw, z, b_raw, a_raw, u_raw, vm_raw, sq, sk, sv = parts
    o_a, s_new, conv_new = _gdn_mixer(qkv_raw, z, b_raw, a_raw, conv_prev, s0, conv_w, a_log, dt_bias,
                                      gdn_norm_g, GDN_BLOCK if first else L)
    o_b, v_rows = _gmlp_mixer(u_raw, vm_raw, ln_g, ln_b, ws, bs, first)
    q = _rms_norm(sq.reshape(bsz, L, SWA_HEADS, SWA_HEAD_DIM), q_norm_g)
    k = _rms_norm(sk.reshape(bsz, L, SWA_KV_HEADS, SWA_HEAD_DIM), k_norm_g)
    v = sv.reshape(bsz, L, SWA_KV_HEADS, SWA_HEAD_DIM)
    if first:
        o_c = _swa_prompt(q, k, v, sinks)
        k_all, v_all = k, v
    else:
        o_c, k_all, v_all = _swa_sample(q, k, v, k_prev, v_prev, sinks)
    x = x + jnp.concatenate([o_a, o_b, o_c], axis=-1) @ w_out
    h = _rms_norm(x, norm2_g)
    x = x + (jax.nn.silu(h @ w_gate) * (h @ w_up)) @ w_down
    return x, (k_all[:, -WINDOW:], v_all[:, -WINDOW:], s_new.astype(x.dtype), conv_new, v_rows)


def setup_inputs(seed: int = 0) -> dict:
    key = jax.random.key(seed)
    ks = jax.random.split(key, 24)
    f32 = jnp.float32

    def nrm(k, shape, scale=1.0):
        return jax.random.normal(k, shape, f32) * scale

    def gain(k, shape):
        return 1.0 + 0.02 * jax.random.normal(k, shape, f32)

    dt = jnp.exp(jax.random.uniform(ks[10], (DEPTH, GDN_HEADS), f32, math.log(1e-3), math.log(1e-1)))
    return {
        'x_prompt': nrm(ks[0], (BATCH, SEQ, D_MODEL)),
        'x_sample': nrm(ks[1], (DEC_BATCH, DEC_SEQ, D_MODEL)),
        'cache_swa_k': nrm(ks[2], (DEPTH, DEC_BATCH, WINDOW, SWA_KV_HEADS, SWA_HEAD_DIM)),
        'cache_swa_v': nrm(ks[3], (DEPTH, DEC_BATCH, WINDOW, SWA_KV_HEADS, SWA_HEAD_DIM)),
        'state_gdn': nrm(ks[4], (DEPTH, DEC_BATCH, GDN_HEADS, GDN_DK, GDN_DV), 0.2),
        'state_gdn_conv': nrm(ks[5], (DEPTH, DEC_BATCH, GDN_CONV - 1, GDN_QKV)),
        'norm1_g': gain(ks[6], (DEPTH, D_MODEL)),
        'w_in': nrm(ks[7], (DEPTH, D_MODEL, IN_COLS), D_MODEL ** -0.5),
        'gdn_conv_w': nrm(ks[8], (DEPTH, GDN_CONV, GDN_QKV), GDN_CONV ** -0.5),
        'gdn_a_log': jnp.log(jax.random.uniform(ks[9], (DEPTH, GDN_HEADS), f32, 1.0, 16.0)),
        'gdn_dt_bias': dt + jnp.log(-jnp.expm1(-dt)),
        'gdn_norm_g': gain(ks[11], (DEPTH, GDN_DV)),
        'mlp_ln_g': gain(ks[12], (DEPTH, MLP_HEADS, MLP_HEAD_DIM)),
        'mlp_ln_b': nrm(ks[13], (DEPTH, MLP_HEADS, MLP_HEAD_DIM), 0.02),
        'mlp_ws': nrm(ks[14], (DEPTH, MLP_HEADS, MLP_CHUNK, MLP_CHUNK), MLP_CHUNK ** -0.5),
        'mlp_bs': gain(ks[15], (DEPTH, MLP_HEADS, MLP_CHUNK)),
        'swa_q_norm_g': gain(ks[16], (DEPTH, SWA_HEAD_DIM)),
        'swa_k_norm_g': gain(ks[17], (DEPTH, SWA_HEAD_DIM)),
        'swa_sinks': nrm(ks[18], (DEPTH, SWA_HEADS), 0.5),
        'w_out': nrm(ks[19], (DEPTH, MIX_WIDTH, D_MODEL), MIX_WIDTH ** -0.5),
        'norm2_g': gain(ks[20], (DEPTH, D_MODEL)),
        'ffn_w_gate': nrm(ks[21], (DEPTH, D_MODEL, D_FF), D_MODEL ** -0.5),
        'ffn_w_up': nrm(ks[22], (DEPTH, D_MODEL, D_FF), D_MODEL ** -0.5),
        'ffn_w_down': nrm(ks[23], (DEPTH, D_FF, D_MODEL), D_FF ** -0.5),
    }


def reference(x_prompt, x_sample, cache_swa_k, cache_swa_v, state_gdn, state_gdn_conv,
              norm1_g, w_in, gdn_conv_w, gdn_a_log, gdn_dt_bias, gdn_norm_g,
              mlp_ln_g, mlp_ln_b, mlp_ws, mlp_bs, swa_q_norm_g, swa_k_norm_g, swa_sinks,
              w_out, norm2_g, ffn_w_gate, ffn_w_up, ffn_w_down):
    yp, ys = x_prompt, x_sample
    p_k, p_v, p_s, p_c = [], [], [], []
    s_k, s_v, s_s, s_c, s_m = [], [], [], [], []
    for l in range(DEPTH):
        wl = (norm1_g[l], w_in[l], gdn_conv_w[l], gdn_a_log[l], gdn_dt_bias[l], gdn_norm_g[l],
              mlp_ln_g[l], mlp_ln_b[l], mlp_ws[l], mlp_bs[l], swa_q_norm_g[l], swa_k_norm_g[l],
              swa_sinks[l], w_out[l], norm2_g[l], ffn_w_gate[l], ffn_w_up[l], ffn_w_down[l])
        conv0 = jnp.zeros((yp.shape[0], GDN_CONV - 1, GDN_QKV), yp.dtype)
        s0 = jnp.zeros((yp.shape[0], GDN_HEADS, GDN_DK, GDN_DV), jnp.float32)
        yp, (k_, v_, st_, cv_, _) = _layer(yp, True, conv0, s0, None, None, *wl)
        p_k.append(k_); p_v.append(v_); p_s.append(st_); p_c.append(cv_)
        ys, (k_, v_, st_, cv_, m_) = _layer(ys, False, state_gdn_conv[l], state_gdn[l],
                                            cache_swa_k[l], cache_swa_v[l], *wl)
        s_k.append(k_); s_v.append(v_); s_s.append(st_); s_c.append(cv_); s_m.append(m_)
    return (yp, ys, jnp.stack(p_k), jnp.stack(p_v), jnp.stack(p_s), jnp.stack(p_c),
            jnp.stack(s_k), jnp.stack(s_v), jnp.stack(s_s), jnp.stack(s_c), jnp.stack(s_m))
```

```python
import functools

import numpy as np
import jax
import jax.numpy as jnp
from jax import lax
from jax.experimental import pallas as pl
from jax.experimental.pallas import tpu as pltpu

F32 = jnp.float32
BF16 = jnp.bfloat16
HIGHEST = lax.Precision.HIGHEST

LANES = 128
CHUNK = 64
EPS = 1e-6
GDN_HEADS, GDN_DK, GDN_DV, GDN_CONV = 4, 128, 128, 4
GDN_QKV = GDN_HEADS * (2 * GDN_DK + GDN_DV)
MLP_HEADS, MLP_HEAD_DIM, MLP_CHUNK = 4, 64, 128
MLP_WIDTH = MLP_HEADS * MLP_HEAD_DIM
SWA_HEADS, SWA_KV_HEADS, SWA_HEAD_DIM, WINDOW = 4, 2, 64, 128
SWA_GROUP = SWA_HEADS // SWA_KV_HEADS
SWA_SCALE = SWA_HEAD_DIM ** -0.5
SWA_Q = SWA_HEADS * SWA_HEAD_DIM
SWA_KV = SWA_KV_HEADS * SWA_HEAD_DIM
GDN_OUT = GDN_HEADS * GDN_DV
MIX_WIDTH = GDN_OUT + MLP_WIDTH + SWA_Q
IN_SIZES = (GDN_QKV, GDN_OUT, GDN_HEADS, GDN_HEADS, MLP_WIDTH, MLP_WIDTH, SWA_Q, SWA_KV, SWA_KV)

C_QKV = 0
C_Z = C_QKV + GDN_QKV
C_U = C_Z + GDN_OUT
C_VM = C_U + MLP_WIDTH
C_SQ = C_VM + MLP_WIDTH
C_SK = C_SQ + SWA_Q
C_SV = C_SK + SWA_KV
C_BA = C_SV + SWA_KV
PROJ_COLS = C_BA + LANES
CONV_PAD = 8
SWA_HEAD_ORDER = (0, 2, 1, 3)

VMEM_LIMIT = 56 * 1024 * 1024


def _dot(a, b):
    return jnp.dot(a, b, preferred_element_type=F32)


def _dot_nt(a, b):
    return lax.dot_general(a, b, (((1,), (1,)), ((), ())), preferred_element_type=F32)


def _dot_tn(a, b):
    return lax.dot_general(a, b, (((0,), (0,)), ((), ())), preferred_element_type=F32)


def _dot_exact(a, b):
    return jnp.dot(a, b, precision=HIGHEST, preferred_element_type=F32)


def _silu(x):
    return x * jax.nn.sigmoid(x)


def _softplus(x):
    return jnp.maximum(x, 0.0) + jnp.log1p(jnp.exp(-jnp.abs(x)))


def _group_mean_matrix(width, group):
    r = lax.broadcasted_iota(jnp.int32, (width, width), 0) // group
    c = lax.broadcasted_iota(jnp.int32, (width, width), 1) // group
    return jnp.where(r == c, 1.0 / group, 0.0).astype(F32)


def _in_proj_kernel(x_ref, g_ref, w_ref, o_ref):
    x = x_ref[...]
    h = x * lax.rsqrt(jnp.mean(x * x, axis=-1, keepdims=True) + EPS) * g_ref[...]
    o_ref[...] = _dot(h.astype(BF16), w_ref[...])


def _in_proj(x2d, g, w, tm):
    t, d = x2d.shape
    n = w.shape[1]
    return pl.pallas_call(
        _in_proj_kernel,
        grid=(t // tm,),
        in_specs=[pl.BlockSpec((tm, d), lambda i: (i, 0)),
                  pl.BlockSpec((1, d), lambda i: (0, 0)),
                  pl.BlockSpec((d, n), lambda i: (0, 0))],
        out_specs=pl.BlockSpec((tm, n), lambda i: (i, 0)),
        out_shape=jax.ShapeDtypeStruct((t, n), F32),
        compiler_params=pltpu.CompilerParams(dimension_semantics=("arbitrary",),
                                             vmem_limit_bytes=VMEM_LIMIT),
        name="in_proj",
    )(x2d, g, w)


def _out_ffn_kernel(mix_ref, x_ref, wo_ref, g_ref, wg_ref, wu_ref, wd_ref, y_ref):
    x1 = x_ref[...] + _dot(mix_ref[...], wo_ref[...])
    h = (x1 * lax.rsqrt(jnp.mean(x1 * x1, axis=-1, keepdims=True) + EPS) * g_ref[...]).astype(BF16)
    act = _silu(_dot(h, wg_ref[...])) * _dot(h, wu_ref[...])
    y_ref[...] = x1 + _dot(act.astype(BF16), wd_ref[...])


def _out_ffn(mix2d, x2d, wo, g2, wg, wu, wd, tm):
    t, d = x2d.shape
    dff = wg.shape[1]
    const = lambda i: (0, 0)
    resident = dict(pipeline_mode=pl.Buffered(1))
    return pl.pallas_call(
        _out_ffn_kernel,
        grid=(t // tm,),
        in_specs=[pl.BlockSpec((tm, d), lambda i: (i, 0)),
                  pl.BlockSpec((tm, d), lambda i: (i, 0)),
                  pl.BlockSpec((d, d), const, **resident),
                  pl.BlockSpec((1, d), const),
                  pl.BlockSpec((d, dff), const, **resident),
                  pl.BlockSpec((d, dff), const, **resident),
                  pl.BlockSpec((dff, d), const, **resident)],
        out_specs=pl.BlockSpec((tm, d), lambda i: (i, 0)),
        out_shape=jax.ShapeDtypeStruct((t, d), F32),
        compiler_params=pltpu.CompilerParams(dimension_semantics=("arbitrary",),
                                             vmem_limit_bytes=VMEM_LIMIT),
        name="out_ffn",
    )(mix2d, x2d, wo, g2, wg, wu, wd)


def _unit_lower_inverse(a_strict):
    n = a_strict.shape[0]
    eye = (lax.broadcasted_iota(jnp.int32, (n, n), 0) == lax.broadcasted_iota(jnp.int32, (n, n), 1)).astype(F32)
    p = -a_strict
    t = eye + p
    steps = int(np.log2(n)) - 1
    for _ in range(steps):
        pb = p.astype(BF16)
        p = _dot(pb, pb)
        t = t + _dot(t.astype(BF16), p.astype(BF16))
    return t


def _mixer_kernel(first, tl, mc, *refs):
    if first:
        (proj_ref, convw_ref, gvec_ref, normg_ref, lng_ref, lnb_ref, ws_ref, bsb_ref, qg_ref, kg_ref, sink_ref,
         mix_ref, s_out, conv_out, k_out, v_out,
         s_scr, xbuf, kbuf, vbuf) = refs
    else:
        (proj_ref, convw_ref, gvec_ref, normg_ref, lng_ref, lnb_ref, ws_ref, bsb_ref, qg_ref, kg_ref, sink_ref,
         conv0_ref, s0_ref, k0_ref, v0_ref,
         mix_ref, s_out, conv_out, k_out, v_out, vrow_out,
         s_scr, xbuf, kbuf, vbuf) = refs
    j = pl.program_id(1)
    nj = pl.num_programs(1)
    nchunk = tl // CHUNK

    @pl.when(j == 0)
    def _init():
        if first:
            s_scr[...] = jnp.zeros_like(s_scr)
            xbuf[0:CONV_PAD, :] = jnp.zeros((CONV_PAD, GDN_QKV), F32)
            kbuf[0:WINDOW, :] = jnp.zeros((WINDOW, SWA_KV), F32)
            vbuf[0:WINDOW, :] = jnp.zeros((WINDOW, SWA_KV), F32)
        else:
            s_scr[...] = s0_ref[...]
            xbuf[0:CONV_PAD, :] = conv0_ref[...]
            kbuf[0:WINDOW, :] = k0_ref[...]
            vbuf[0:WINDOW, :] = v0_ref[...]

    xbuf[CONV_PAD:CONV_PAD + tl, :] = proj_ref[:, C_QKV:C_QKV + GDN_QKV]
    base = CONV_PAD - (GDN_CONV - 1)
    acc = xbuf[base:base + tl, :] * convw_ref[0:1, :]
    for w in range(1, GDN_CONV):
        acc = acc + xbuf[base + w:base + w + tl, :] * convw_ref[w:w + 1, :]
    qkv = _silu(acc)
    conv_tail = xbuf[tl:tl + CONV_PAD, :]
    xbuf[0:CONV_PAD, :] = conv_tail

    ba = proj_ref[:, C_BA:C_BA + LANES]
    beta_all = jax.nn.sigmoid(ba)
    g_all = -jnp.exp(gvec_ref[0:1, :]) * _softplus(ba + gvec_ref[1:2, :])
    r_i = lax.broadcasted_iota(jnp.int32, (tl, tl), 0)
    c_i = lax.broadcasted_iota(jnp.int32, (tl, tl), 1)
    cum_mat = jnp.where((r_i // CHUNK == c_i // CHUNK) & (c_i <= r_i), 1.0, 0.0).astype(F32)
    gc_all = _dot_exact(cum_mat, g_all)
    sel = (lax.broadcasted_iota(jnp.int32, (8, LANES), 1)
           == lax.broadcasted_iota(jnp.int32, (8, LANES), 0) + GDN_HEADS).astype(F32)
    gc_rows = lax.dot_general(sel, gc_all, (((1,), (1,)), ((), ())), precision=HIGHEST,
                              preferred_element_type=F32)

    ci = lax.broadcasted_iota(jnp.int32, (CHUNK, CHUNK), 0)
    cj = lax.broadcasted_iota(jnp.int32, (CHUNK, CHUNK), 1)
    tri_incl = cj <= ci
    tri_strict = cj < ci

    for h in range(GDN_HEADS):
        q_h = qkv[:, h * GDN_DK:(h + 1) * GDN_DK]
        k_h = qkv[:, GDN_OUT + h * GDN_DK:GDN_OUT + (h + 1) * GDN_DK]
        v_h = qkv[:, 2 * GDN_OUT + h * GDN_DV:2 * GDN_OUT + (h + 1) * GDN_DV]
        q_h = q_h * lax.rsqrt(jnp.sum(q_h * q_h, axis=-1, keepdims=True) + EPS) * (GDN_DK ** -0.5)
        k_h = k_h * lax.rsqrt(jnp.sum(k_h * k_h, axis=-1, keepdims=True) + EPS)
        z_h = proj_ref[:, C_Z + h * GDN_DV:C_Z + (h + 1) * GDN_DV]
        state = s_scr[h]
        for c in range(nchunk):
            rows = slice(c * CHUNK, (c + 1) * CHUNK)
            qc, kc, vc = q_h[rows], k_h[rows], v_h[rows]
            bcol = beta_all[rows, h:h + 1]
            gcol = gc_all[rows, GDN_HEADS + h:GDN_HEADS + h + 1]
            grow = gc_rows[h:h + 1, c * CHUNK:(c + 1) * CHUNK]
            decay = jnp.exp(jnp.where(tri_incl, gcol - grow, -jnp.inf))
            kb = kc * bcol
            kcb = kc.astype(BF16)
            a_low = jnp.where(tri_strict, _dot_nt(kb.astype(BF16), kcb) * decay, 0.0)
            attn = jnp.where(tri_incl, _dot_nt(qc.astype(BF16), kcb) * decay, 0.0)
            tinv = _unit_lower_inverse(a_low)
            rhs = jnp.concatenate([vc * bcol, kb * jnp.exp(gcol)], axis=-1)
            sol = _dot(tinv.astype(BF16), rhs.astype(BF16))
            u_c, w_c = sol[:, :GDN_DV], sol[:, GDN_DV:]
            glast = gcol[CHUNK - 1:CHUNK, :]
            sb = state.astype(BF16)
            v_new = u_c - _dot(w_c.astype(BF16), sb)
            vnb = v_new.astype(BF16)
            o = _dot((qc * jnp.exp(gcol)).astype(BF16), sb) + _dot(attn.astype(BF16), vnb)
            kd = kc * jnp.exp(glast - gcol)
            state = state * jnp.exp(glast) + _dot_tn(kd.astype(BF16), vnb)
            o = o * lax.rsqrt(jnp.mean(o * o, axis=-1, keepdims=True) + EPS) * normg_ref[...]
            o = o * _silu(z_h[rows])
            mix_ref[rows, h * GDN_DV:(h + 1) * GDN_DV] = o.astype(mix_ref.dtype)
        s_scr[h] = state

    mean_mlp = _group_mean_matrix(MLP_WIDTH, MLP_HEAD_DIM)
    u_act = jax.nn.gelu(proj_ref[:, C_U:C_U + MLP_WIDTH])
    v_act = jax.nn.gelu(proj_ref[:, C_VM:C_VM + MLP_WIDTH])
    v_cent = v_act - _dot_exact(v_act, mean_mlp)
    v_var = _dot_exact(v_cent * v_cent, mean_mlp)
    v_norm = v_cent * lax.rsqrt(v_var + EPS) * lng_ref[...] + lnb_ref[...]
    if not first:
        vrow_out[...] = v_norm
    mi = lax.broadcasted_iota(jnp.int32, (mc, mc), 0)
    mj = lax.broadcasted_iota(jnp.int32, (mc, mc), 1)
    block_causal = (mi // CHUNK) >= (mj // CHUNK)
    low_half = lax.broadcasted_iota(jnp.int32, (mc, LANES), 1) < MLP_HEAD_DIM
    for m in range(tl // mc):
        rows = slice(m * mc, (m + 1) * mc)
        for p in range(MLP_WIDTH // LANES):
            lanes = slice(p * LANES, (p + 1) * LANES)
            vpair = v_norm[rows, lanes].astype(BF16)
            w0 = jnp.where(block_causal, ws_ref[2 * p], 0.0).astype(BF16)
            w1 = jnp.where(block_causal, ws_ref[2 * p + 1], 0.0).astype(BF16)
            s = jnp.where(low_half, _dot(w0, vpair), _dot(w1, vpair)) + bsb_ref[:, lanes]
            mix_ref[rows, GDN_OUT + p * LANES:GDN_OUT + (p + 1) * LANES] = (u_act[rows, lanes] * s).astype(mix_ref.dtype)

    mean_head = _group_mean_matrix(SWA_Q, SWA_HEAD_DIM)
    sq = proj_ref[:, C_SQ:C_SQ + SWA_Q]
    sk = proj_ref[:, C_SK:C_SK + SWA_KV]
    qn = sq * lax.rsqrt(_dot_exact(sq * sq, mean_head) + EPS) * qg_ref[...]
    kn = sk * lax.rsqrt(_dot_exact(sk * sk, mean_head[:SWA_KV, :SWA_KV]) + EPS) * kg_ref[...]
    kbuf[WINDOW:WINDOW + tl, :] = kn
    vbuf[WINDOW:WINDOW + tl, :] = proj_ref[:, C_SV:C_SV + SWA_KV]
    nkeys = WINDOW + CHUNK
    top_rows = lax.broadcasted_iota(jnp.int32, (2 * CHUNK, 1), 0) < CHUNK
    lane_lo_q = lax.broadcasted_iota(jnp.int32, (CHUNK, LANES), 1) < SWA_HEAD_DIM
    key_off = lax.broadcasted_iota(jnp.int32, (2 * CHUNK, nkeys), 1)
    for c in range(nchunk):
        rows = slice(c * CHUNK, (c + 1) * CHUNK)
        keys = kbuf[c * CHUNK:c * CHUNK + nkeys, :].astype(BF16)
        vals = vbuf[c * CHUNK:c * CHUNK + nkeys, :].astype(BF16)
        for g in range(SWA_GROUP):
            qg = qn[rows, g * LANES:(g + 1) * LANES]
            qstack = jnp.concatenate([jnp.where(lane_lo_q, qg, 0.0), jnp.where(lane_lo_q, 0.0, qg)], axis=0)
            s = _dot_nt(qstack.astype(BF16), keys) * SWA_SCALE
            if first:
                key_pos = j * tl + (c * CHUNK - WINDOW) + key_off
                s = jnp.where(key_pos >= 0, s, -jnp.inf)
            sink = jnp.where(top_rows, sink_ref[g], sink_ref[SWA_GROUP + g])
            mx = jnp.maximum(jnp.max(s, axis=-1, keepdims=True), sink)
            e = jnp.exp(s - mx)
            den = jnp.sum(e, axis=-1, keepdims=True) + jnp.exp(sink - mx)
            pv = _dot((e / den).astype(BF16), vals)
            o = jnp.where(lane_lo_q, pv[:CHUNK], pv[CHUNK:])
            col = GDN_OUT + MLP_WIDTH + g * LANES
            mix_ref[rows, col:col + LANES] = o.astype(mix_ref.dtype)
    k_tail = kbuf[tl:tl + WINDOW, :]
    v_tail = vbuf[tl:tl + WINDOW, :]
    if tl >= WINDOW:
        kbuf[0:WINDOW, :] = k_tail
        vbuf[0:WINDOW, :] = v_tail

    @pl.when(j == nj - 1)
    def _final():
        s_out[...] = s_scr[...]
        conv_out[...] = conv_tail
        k_out[...] = k_tail
        v_out[...] = v_tail


def _mixers(first, proj3d, lw, init, tl):
    b, l, _ = proj3d.shape
    mc = MLP_CHUNK if first else CHUNK
    nj = l // tl
    full = lambda shape: pl.BlockSpec(shape, lambda bi, ji: (0,) * len(shape))
    per_b = lambda shape: pl.BlockSpec((None,) + shape, lambda bi, ji: (bi,) + (0,) * len(shape))
    in_specs = [pl.BlockSpec((None, tl, PROJ_COLS), lambda bi, ji: (bi, ji, 0)),
                full((GDN_CONV, GDN_QKV)), full((2, LANES)), full((1, GDN_DV)),
                full((1, MLP_WIDTH)), full((1, MLP_WIDTH)), full((MLP_HEADS, mc, mc)), full((mc, MLP_WIDTH)),
                full((1, SWA_Q)), full((1, SWA_KV)),
                pl.BlockSpec(memory_space=pltpu.SMEM)]
    args = [proj3d, lw["conv_w"], lw["gvec"], lw["gdn_norm_g"], lw["ln_g"], lw["ln_b"],
            lw["ws"] if first else lw["ws"][:, :mc, :mc], lw["bs_tile"][:mc], lw["q_g"], lw["k_g"], lw["sinks"]]
    out_specs = [pl.BlockSpec((None, tl, MIX_WIDTH), lambda bi, ji: (bi, ji, 0)),
                 per_b((GDN_HEADS, GDN_DK, GDN_DV)), per_b((CONV_PAD, GDN_QKV)),
                 per_b((WINDOW, SWA_KV)), per_b((WINDOW, SWA_KV))]
    out_shape = [jax.ShapeDtypeStruct((b, l, MIX_WIDTH), BF16),
                 jax.ShapeDtypeStruct((b, GDN_HEADS, GDN_DK, GDN_DV), F32),
                 jax.ShapeDtypeStruct((b, CONV_PAD, GDN_QKV), F32),
                 jax.ShapeDtypeStruct((b, WINDOW, SWA_KV), F32),
                 jax.ShapeDtypeStruct((b, WINDOW, SWA_KV), F32)]
    if not first:
        conv0, s0, k0, v0 = init
        in_specs += [per_b((CONV_PAD, GDN_QKV)), per_b((GDN_HEADS, GDN_DK, GDN_DV)),
                     per_b((WINDOW, SWA_KV)), per_b((WINDOW, SWA_KV))]
        args += [conv0, s0, k0, v0]
        out_specs.append(pl.BlockSpec((None, tl, MLP_WIDTH), lambda bi, ji: (bi, ji, 0)))
        out_shape.append(jax.ShapeDtypeStruct((b, l, MLP_WIDTH), F32))
    return pl.pallas_call(
        functools.partial(_mixer_kernel, first, tl, mc),
        grid=(b, nj),
        in_specs=in_specs,
        out_specs=out_specs,
        out_shape=out_shape,
        scratch_shapes=[pltpu.VMEM((GDN_HEADS, GDN_DK, GDN_DV), F32),
                        pltpu.VMEM((CONV_PAD + tl, GDN_QKV), F32),
                        pltpu.VMEM((WINDOW + tl, SWA_KV), F32),
                        pltpu.VMEM((WINDOW + tl, SWA_KV), F32)],
        compiler_params=pltpu.CompilerParams(dimension_semantics=("arbitrary", "arbitrary"),
                                             vmem_limit_bytes=VMEM_LIMIT),
        name="mixers_prompt" if first else "mixers_sample",
    )(*args)


def _layer_weights(l, w_in, conv_w, a_log, dt_bias, gdn_norm_g, ln_g, ln_b, ws, bs, q_g, k_g, sinks,
                   w_out, norm1_g, norm2_g, w_gate, w_up, w_down):
    d = w_in.shape[1]
    offs = np.cumsum((0,) + IN_SIZES)
    cols = [w_in[l][:, offs[i]:offs[i + 1]] for i in range(len(IN_SIZES))]
    qkv, z, b_raw, a_raw, u, vm, sq, sk, sv = cols
    order = np.asarray(SWA_HEAD_ORDER)
    sq = sq.reshape(d, SWA_HEADS, SWA_HEAD_DIM)[:, order].reshape(d, SWA_Q)
    ba = jnp.concatenate([b_raw, a_raw, jnp.zeros((d, LANES - 2 * GDN_HEADS), F32)], axis=1)
    w_in_r = jnp.concatenate([qkv, z, u, vm, sq, sk, sv, ba], axis=1).astype(BF16)
    wo = w_out[l]
    wo_swa = wo[GDN_OUT + MLP_WIDTH:].reshape(SWA_HEADS, SWA_HEAD_DIM, d)[order].reshape(SWA_Q, d)
    wo_r = jnp.concatenate([wo[:GDN_OUT + MLP_WIDTH], wo_swa], axis=0).astype(BF16)
    pad = jnp.zeros((GDN_HEADS,), F32)
    lane_vec = lambda v: jnp.concatenate([pad, v, jnp.zeros((LANES - 2 * GDN_HEADS,), F32)])
    return dict(
        w_in=w_in_r, w_out=wo_r,
        norm1_g=norm1_g[l][None, :], norm2_g=norm2_g[l][None, :],
        conv_w=conv_w[l],
        gvec=jnp.stack([lane_vec(a_log[l]), lane_vec(dt_bias[l])]),
        gdn_norm_g=gdn_norm_g[l][None, :],
        ln_g=ln_g[l].reshape(1, MLP_WIDTH), ln_b=ln_b[l].reshape(1, MLP_WIDTH),
        ws=ws[l],
        bs_tile=jnp.repeat(bs[l].T, MLP_HEAD_DIM, axis=1),
        q_g=jnp.tile(q_g[l], SWA_HEADS)[None, :], k_g=jnp.tile(k_g[l], SWA_KV_HEADS)[None, :],
        sinks=sinks[l],
        w_gate=w_gate[l].astype(BF16), w_up=w_up[l].astype(BF16), w_down=w_down[l].astype(BF16),
    )


def _layer(x, first, lw, init, tm, tl):
    b, l, d = x.shape
    x2d = x.reshape(b * l, d)
    proj = _in_proj(x2d, lw["norm1_g"], lw["w_in"], tm).reshape(b, l, PROJ_COLS)
    outs = _mixers(first, proj, lw, init, tl)
    mix = outs[0].reshape(b * l, MIX_WIDTH)
    y = _out_ffn(mix, x2d, lw["w_out"], lw["norm2_g"], lw["w_gate"], lw["w_up"], lw["w_down"], tm)
    return y.reshape(b, l, d), outs[1:]


def kernel(x_prompt, x_sample, cache_swa_k, cache_swa_v, state_gdn, state_gdn_conv, norm1_g, w_in, gdn_conv_w, gdn_a_log, gdn_dt_bias, gdn_norm_g, mlp_ln_g, mlp_ln_b, mlp_ws, mlp_bs, swa_q_norm_g, swa_k_norm_g, swa_sinks, w_out, norm2_g, ffn_w_gate, ffn_w_up, ffn_w_down):
    depth = w_in.shape[0]
    bs_dec, dec_seq = x_sample.shape[:2]
    bp = x_prompt.shape[0]
    yp, ys = x_prompt, x_sample
    p_k, p_v, p_s, p_c = [], [], [], []
    s_k, s_v, s_s, s_c, s_m = [], [], [], [], []
    kv_shape = (WINDOW, SWA_KV_HEADS, SWA_HEAD_DIM)
    for l in range(depth):
        lw = _layer_weights(l, w_in, gdn_conv_w, gdn_a_log, gdn_dt_bias, gdn_norm_g, mlp_ln_g, mlp_ln_b,
                            mlp_ws, mlp_bs, swa_q_norm_g, swa_k_norm_g, swa_sinks, w_out, norm1_g, norm2_g,
                            ffn_w_gate, ffn_w_up, ffn_w_down)
        yp, (st, cv, k_, v_) = _layer(yp, True, lw, None, 512, 256)
        p_k.append(k_.reshape((bp,) + kv_shape))
        p_v.append(v_.reshape((bp,) + kv_shape))
        p_s.append(st)
        p_c.append(cv[:, CONV_PAD - (GDN_CONV - 1):])
        conv0 = jnp.pad(state_gdn_conv[l], ((0, 0), (CONV_PAD - (GDN_CONV - 1), 0), (0, 0)))
        init = (conv0, state_gdn[l], cache_swa_k[l].reshape(bs_dec, WINDOW, SWA_KV),
                cache_swa_v[l].reshape(bs_dec, WINDOW, SWA_KV))
        ys, (st, cv, k_, v_, vr) = _layer(ys, False, lw, init, 512, dec_seq)
        s_k.append(k_.reshape((bs_dec,) + kv_shape))
        s_v.append(v_.reshape((bs_dec,) + kv_shape))
        s_s.append(st)
        s_c.append(cv[:, CONV_PAD - (GDN_CONV - 1):])
        s_m.append(vr.reshape(bs_dec, dec_seq, MLP_HEADS, MLP_HEAD_DIM))
    return (yp, ys, jnp.stack(p_k), jnp.stack(p_v), jnp.stack(p_s), jnp.stack(p_c),
            jnp.stack(s_k), jnp.stack(s_v), jnp.stack(s_s), jnp.stack(s_c), jnp.stack(s_m))
```

```python
import functools

import numpy as np
import jax
import jax.numpy as jnp
from jax import lax
from jax.experimental import pallas as pl
from jax.experimental.pallas import tpu as pltpu

F32 = jnp.float32
BF16 = jnp.bfloat16
HIGHEST = lax.Precision.HIGHEST

LANES = 128
CHUNK = 64
EPS = 1e-6
GDN_HEADS, GDN_DK, GDN_DV, GDN_CONV = 4, 128, 128, 4
GDN_PAIRS = GDN_HEADS // 2
GDN_QKV = GDN_HEADS * (2 * GDN_DK + GDN_DV)
MLP_HEADS, MLP_HEAD_DIM, MLP_CHUNK = 4, 64, 128
MLP_WIDTH = MLP_HEADS * MLP_HEAD_DIM
SWA_HEADS, SWA_KV_HEADS, SWA_HEAD_DIM, WINDOW = 4, 2, 64, 128
SWA_GROUP = SWA_HEADS // SWA_KV_HEADS
SWA_SCALE = SWA_HEAD_DIM ** -0.5
SWA_Q = SWA_HEADS * SWA_HEAD_DIM
SWA_KV = SWA_KV_HEADS * SWA_HEAD_DIM
GDN_OUT = GDN_HEADS * GDN_DV
MIX_WIDTH = GDN_OUT + MLP_WIDTH + SWA_Q
IN_SIZES = (GDN_QKV, GDN_OUT, GDN_HEADS, GDN_HEADS, MLP_WIDTH, MLP_WIDTH, SWA_Q, SWA_KV, SWA_KV)

C_QKV = 0
C_Z = C_QKV + GDN_QKV
C_U = C_Z + GDN_OUT
C_VM = C_U + MLP_WIDTH
C_SQ = C_VM + MLP_WIDTH
C_SK = C_SQ + SWA_Q
C_SV = C_SK + SWA_KV
C_BA = C_SV + SWA_KV
PROJ_COLS = C_BA + LANES
CONV_PAD = 8
SWA_HEAD_ORDER = (0, 2, 1, 3)

VMEM_LIMIT = 56 * 1024 * 1024


def _dot(a, b):
    return jnp.dot(a, b, preferred_element_type=F32)


def _dot_nt(a, b):
    return lax.dot_general(a, b, (((1,), (1,)), ((), ())), preferred_element_type=F32)


def _dot_tn(a, b):
    return lax.dot_general(a, b, (((0,), (0,)), ((), ())), preferred_element_type=F32)


def _dot_exact(a, b):
    return jnp.dot(a, b, precision=HIGHEST, preferred_element_type=F32)


def _dot_split(x, m):
    hi = x.astype(BF16)
    lo = (x - hi.astype(F32)).astype(BF16)
    return _dot(hi, m) + _dot(lo, m)


def _silu(x):
    return x * jax.nn.sigmoid(x)


def _softplus(x):
    return jnp.maximum(x, 0.0) + jnp.log1p(jnp.exp(-jnp.abs(x)))


def _group_mean_matrix(width, group):
    r = lax.broadcasted_iota(jnp.int32, (width, width), 0) // group
    c = lax.broadcasted_iota(jnp.int32, (width, width), 1) // group
    return jnp.where(r == c, 1.0 / group, 0.0).astype(BF16)


def _in_proj_kernel(x_ref, g_ref, w_ref, o_ref):
    x = x_ref[...]
    h = x * lax.rsqrt(jnp.mean(x * x, axis=-1, keepdims=True) + EPS) * g_ref[...]
    o_ref[...] = _dot(h.astype(BF16), w_ref[...])


def _in_proj(x2d, g, w, tm):
    t, d = x2d.shape
    n = w.shape[1]
    return pl.pallas_call(
        _in_proj_kernel,
        grid=(t // tm,),
        in_specs=[pl.BlockSpec((tm, d), lambda i: (i, 0)),
                  pl.BlockSpec((1, d), lambda i: (0, 0)),
                  pl.BlockSpec((d, n), lambda i: (0, 0))],
        out_specs=pl.BlockSpec((tm, n), lambda i: (i, 0)),
        out_shape=jax.ShapeDtypeStruct((t, n), F32),
        compiler_params=pltpu.CompilerParams(dimension_semantics=("arbitrary",),
                                             vmem_limit_bytes=VMEM_LIMIT),
        name="in_proj",
    )(x2d, g, w)


def _out_ffn_kernel(mix_ref, x_ref, wo_ref, g_ref, wg_ref, wu_ref, wd_ref, y_ref):
    x1 = x_ref[...] + _dot(mix_ref[...], wo_ref[...])
    h = (x1 * lax.rsqrt(jnp.mean(x1 * x1, axis=-1, keepdims=True) + EPS) * g_ref[...]).astype(BF16)
    act = _silu(_dot(h, wg_ref[...])) * _dot(h, wu_ref[...])
    y_ref[...] = x1 + _dot(act.astype(BF16), wd_ref[...])


def _out_ffn(mix2d, x2d, wo, g2, wg, wu, wd, tm):
    t, d = x2d.shape
    dff = wg.shape[1]
    const = lambda i: (0, 0)
    resident = dict(pipeline_mode=pl.Buffered(1))
    return pl.pallas_call(
        _out_ffn_kernel,
        grid=(t // tm,),
        in_specs=[pl.BlockSpec((tm, d), lambda i: (i, 0)),
                  pl.BlockSpec((tm, d), lambda i: (i, 0)),
                  pl.BlockSpec((d, d), const, **resident),
                  pl.BlockSpec((1, d), const),
                  pl.BlockSpec((d, dff), const, **resident),
                  pl.BlockSpec((d, dff), const, **resident),
                  pl.BlockSpec((dff, d), const, **resident)],
        out_specs=pl.BlockSpec((tm, d), lambda i: (i, 0)),
        out_shape=jax.ShapeDtypeStruct((t, d), F32),
        compiler_params=pltpu.CompilerParams(dimension_semantics=("arbitrary",),
                                             vmem_limit_bytes=VMEM_LIMIT),
        name="out_ffn",
    )(mix2d, x2d, wo, g2, wg, wu, wd)


def _block_diag(x, lo_lanes):
    return jnp.concatenate([jnp.where(lo_lanes, x, 0.0), jnp.where(lo_lanes, 0.0, x)], axis=0).astype(BF16)


def _unit_lower_inverse_wide(a_list, eye_w, lo_lanes):
    ps = [-a for a in a_list]
    ts = [eye_w + p for p in ps]
    ps = [_dot(p.astype(BF16), _block_diag(p, lo_lanes)) for p in ps]
    for _ in range(int(np.log2(CHUNK)) - 2):
        ys = [_dot(jnp.concatenate([t, p], axis=0).astype(BF16), _block_diag(p, lo_lanes)) for t, p in zip(ts, ps)]
        ts = [t + y[:CHUNK] for t, y in zip(ts, ys)]
        ps = [y[CHUNK:] for y in ys]
    return [t + _dot(t.astype(BF16), _block_diag(p, lo_lanes)) for t, p in zip(ts, ps)]


def _mixer_kernel(first, tl, mc, *refs):
    if first:
        (proj_ref, convw_ref, gvec_ref, normg_ref, lng_ref, lnb_ref, ws_ref, bsb_ref, qg_ref, kg_ref, sink_ref,
         mix_ref, s_out, conv_out, k_out, v_out,
         s_scr, xbuf, kbuf, vbuf) = refs
    else:
        (proj_ref, convw_ref, gvec_ref, normg_ref, lng_ref, lnb_ref, ws_ref, bsb_ref, qg_ref, kg_ref, sink_ref,
         conv0_ref, s0_ref, k0_ref, v0_ref,
         mix_ref, s_out, conv_out, k_out, v_out, vrow_out,
         s_scr, xbuf, kbuf, vbuf) = refs
    j = pl.program_id(1)
    nj = pl.num_programs(1)
    nchunk = tl // CHUNK

    @pl.when(j == 0)
    def _init():
        if first:
            s_scr[...] = jnp.zeros_like(s_scr)
            xbuf[0:CONV_PAD, :] = jnp.zeros((CONV_PAD, GDN_QKV), F32)
            kbuf[0:WINDOW, :] = jnp.zeros((WINDOW, SWA_KV), F32)
            vbuf[0:WINDOW, :] = jnp.zeros((WINDOW, SWA_KV), F32)
        else:
            for h in range(GDN_HEADS):
                s_scr[h * GDN_DK:(h + 1) * GDN_DK, :] = s0_ref[h]
            xbuf[0:CONV_PAD, :] = conv0_ref[...]
            kbuf[0:WINDOW, :] = k0_ref[...]
            vbuf[0:WINDOW, :] = v0_ref[...]

    xbuf[CONV_PAD:CONV_PAD + tl, :] = proj_ref[:, C_QKV:C_QKV + GDN_QKV]
    base = CONV_PAD - (GDN_CONV - 1)
    acc = xbuf[base:base + tl, :] * convw_ref[0:1, :]
    for w in range(1, GDN_CONV):
        acc = acc + xbuf[base + w:base + w + tl, :] * convw_ref[w:w + 1, :]
    qkv = _silu(acc)
    conv_tail = xbuf[tl:tl + CONV_PAD, :]
    xbuf[0:CONV_PAD, :] = conv_tail

    ba = proj_ref[:, C_BA:C_BA + LANES]
    beta_all = jax.nn.sigmoid(ba)
    g_all = -jnp.exp(gvec_ref[0:1, :]) * _softplus(ba + gvec_ref[1:2, :])
    r_i = lax.broadcasted_iota(jnp.int32, (tl, tl), 0)
    c_i = lax.broadcasted_iota(jnp.int32, (tl, tl), 1)
    cum_mat = jnp.where((r_i // CHUNK == c_i // CHUNK) & (c_i <= r_i), 1.0, 0.0).astype(F32)
    gc_all = _dot_exact(cum_mat, g_all)

    qn, kn, vv = [], [], []
    for h in range(GDN_HEADS):
        q_h = qkv[:, h * GDN_DK:(h + 1) * GDN_DK]
        k_h = qkv[:, GDN_OUT + h * GDN_DK:GDN_OUT + (h + 1) * GDN_DK]
        qn.append(q_h * lax.rsqrt(jnp.sum(q_h * q_h, axis=-1, keepdims=True) + EPS) * (GDN_DK ** -0.5))
        kn.append(k_h * lax.rsqrt(jnp.sum(k_h * k_h, axis=-1, keepdims=True) + EPS))
        vv.append(qkv[:, 2 * GDN_OUT + h * GDN_DV:2 * GDN_OUT + (h + 1) * GDN_DV])

    row_w = lax.broadcasted_iota(jnp.int32, (CHUNK, LANES), 0)
    lane_w = lax.broadcasted_iota(jnp.int32, (CHUNK, LANES), 1)
    lane_in = lane_w & (CHUNK - 1)
    lo_w = lane_w < CHUNK
    incl_w = lane_in <= row_w
    strict_w = lane_in < row_w
    eye_w = jnp.where(lane_in == row_w, 1.0, 0.0).astype(F32)
    top_st = lax.broadcasted_iota(jnp.int32, (2 * CHUNK, LANES), 0) < CHUNK
    head0_rows = (lax.broadcasted_iota(jnp.int32, (4 * CHUNK, LANES), 0) & CHUNK) == 0
    zero_blk = jnp.zeros((CHUNK, GDN_DK), F32)
    ones8 = jnp.ones((8, LANES), F32)

    items = []
    for c in range(nchunk):
        rows = slice(c * CHUNK, (c + 1) * CHUNK)
        gc_c = gc_all[rows]
        g_heads = jnp.concatenate([jnp.where(lane_w == GDN_HEADS + h, gc_c, 0.0) for h in range(GDN_HEADS)], axis=0)
        grow_all = lax.dot_general(ones8, g_heads, (((1,), (1,)), ((), ())), precision=HIGHEST,
                                   preferred_element_type=F32)
        for p in range(GDN_PAIRS):
            h0, h1 = 2 * p, 2 * p + 1
            q0, q1, k0, k1, v0, v1 = qn[h0][rows], qn[h1][rows], kn[h0][rows], kn[h1][rows], vv[h0][rows], vv[h1][rows]
            b0, b1 = beta_all[rows, h0:h0 + 1], beta_all[rows, h1:h1 + 1]
            g0 = gc_c[:, GDN_HEADS + h0:GDN_HEADS + h0 + 1]
            g1 = gc_c[:, GDN_HEADS + h1:GDN_HEADS + h1 + 1]
            kb0, kb1 = k0 * b0, k1 * b1
            lhs = jnp.concatenate([jnp.concatenate([kb0, kb1], axis=-1),
                                   jnp.concatenate([q0, q1], axis=-1)], axis=0).astype(BF16)
            rhs = jnp.concatenate([jnp.concatenate([k0, zero_blk], axis=-1),
                                   jnp.concatenate([zero_blk, k1], axis=-1)], axis=0).astype(BF16)
            x = _dot_nt(lhs, rhs)
            grow_w = grow_all[0:1, p * LANES:(p + 1) * LANES]
            decay = jnp.exp(jnp.where(incl_w, jnp.where(lo_w, g0, g1) - grow_w, -jnp.inf))
            e0, e1 = jnp.exp(g0), jnp.exp(g1)
            gl0, gl1 = g0[CHUNK - 1:CHUNK, :], g1[CHUNK - 1:CHUNK, :]
            items.append(dict(
                c=c, p=p,
                a=jnp.where(strict_w, x[:CHUNK] * decay, 0.0),
                attn=x[CHUNK:] * decay,
                rhs=jnp.concatenate([jnp.concatenate([v0 * b0, kb0 * e0], axis=-1),
                                     jnp.concatenate([v1 * b1, kb1 * e1], axis=-1)], axis=0).astype(BF16),
                qg=jnp.concatenate([q0 * e0, q1 * e1], axis=0),
                kd=jnp.concatenate([k0 * jnp.exp(gl0 - g0), k1 * jnp.exp(gl1 - g1)], axis=0),
                scale=jnp.concatenate([jnp.broadcast_to(jnp.exp(gl0), (GDN_DK, 1)),
                                       jnp.broadcast_to(jnp.exp(gl1), (GDN_DK, 1))], axis=0)))
    tinvs = _unit_lower_inverse_wide([it["a"] for it in items], eye_w, lo_w)
    for it, tinv in zip(items, tinvs):
        it["sol"] = _dot(_block_diag(tinv, lo_w), it["rhs"])

    pair_rows = [slice(p * 2 * GDN_DK, (p + 1) * 2 * GDN_DK) for p in range(GDN_PAIRS)]
    states = [s_scr[r, :] for r in pair_rows]

    def gdn_scan_step(it):
        p, rows = it["p"], slice(it["c"] * CHUNK, (it["c"] + 1) * CHUNK)
        state = states[p]
        wq = jnp.concatenate([it["sol"][:, GDN_DV:], it["qg"]], axis=0)
        lhs = jnp.concatenate([jnp.where(head0_rows, wq, 0.0), jnp.where(head0_rows, 0.0, wq)], axis=-1)
        y = _dot(lhs.astype(BF16), state.astype(BF16))
        v_new = (it["sol"][:, :GDN_DV] - y[:2 * CHUNK]).astype(BF16)
        o = y[2 * CHUNK:] + _dot(_block_diag(it["attn"], lo_w), v_new)
        kd_w = jnp.concatenate([jnp.where(top_st, it["kd"], 0.0), jnp.where(top_st, 0.0, it["kd"])], axis=-1)
        states[p] = state * it["scale"] + _dot_tn(kd_w.astype(BF16), v_new)
        o = o * lax.rsqrt(jnp.mean(o * o, axis=-1, keepdims=True) + EPS) * normg_ref[...]
        for i in range(2):
            h = 2 * p + i
            z = proj_ref[rows, C_Z + h * GDN_DV:C_Z + (h + 1) * GDN_DV]
            mix_ref[rows, h * GDN_DV:(h + 1) * GDN_DV] = (o[i * CHUNK:(i + 1) * CHUNK] * _silu(z)).astype(mix_ref.dtype)

    mean_mlp = _group_mean_matrix(MLP_WIDTH, MLP_HEAD_DIM)
    u_act = jax.nn.gelu(proj_ref[:, C_U:C_U + MLP_WIDTH])
    v_act = jax.nn.gelu(proj_ref[:, C_VM:C_VM + MLP_WIDTH])
    v_cent = v_act - _dot_split(v_act, mean_mlp)
    v_var = _dot_split(v_cent * v_cent, mean_mlp)
    v_norm = v_cent * lax.rsqrt(v_var + EPS) * lng_ref[...] + lnb_ref[...]
    if not first:
        vrow_out[...] = v_norm
    mi = lax.broadcasted_iota(jnp.int32, (mc, mc), 0)
    mj = lax.broadcasted_iota(jnp.int32, (mc, mc), 1)
    block_causal = (mi // CHUNK) >= (mj // CHUNK)
    low_half = lax.broadcasted_iota(jnp.int32, (mc, LANES), 1) < MLP_HEAD_DIM

    def gmlp_chunk(m):
        rows = slice(m * mc, (m + 1) * mc)
        for p in range(MLP_WIDTH // LANES):
            lanes = slice(p * LANES, (p + 1) * LANES)
            vpair = v_norm[rows, lanes].astype(BF16)
            w0 = jnp.where(block_causal, ws_ref[2 * p], 0.0).astype(BF16)
            w1 = jnp.where(block_causal, ws_ref[2 * p + 1], 0.0).astype(BF16)
            s = jnp.where(low_half, _dot(w0, vpair), _dot(w1, vpair)) + bsb_ref[:, lanes]
            mix_ref[rows, GDN_OUT + p * LANES:GDN_OUT + (p + 1) * LANES] = (u_act[rows, lanes] * s).astype(mix_ref.dtype)

    mean_head = _group_mean_matrix(SWA_Q, SWA_HEAD_DIM)
    sq = proj_ref[:, C_SQ:C_SQ + SWA_Q]
    sk = proj_ref[:, C_SK:C_SK + SWA_KV]
    qn_swa = sq * lax.rsqrt(_dot_split(sq * sq, mean_head) + EPS) * qg_ref[...]
    kn_swa = sk * lax.rsqrt(_dot_split(sk * sk, mean_head[:SWA_KV, :SWA_KV]) + EPS) * kg_ref[...]
    kbuf[WINDOW:WINDOW + tl, :] = kn_swa
    vbuf[WINDOW:WINDOW + tl, :] = proj_ref[:, C_SV:C_SV + SWA_KV]
    nkeys = WINDOW + CHUNK
    top_rows = lax.broadcasted_iota(jnp.int32, (2 * CHUNK, 1), 0) < CHUNK
    key_off = lax.broadcasted_iota(jnp.int32, (2 * CHUNK, nkeys), 1)

    def swa_chunk(c):
        rows = slice(c * CHUNK, (c + 1) * CHUNK)
        keys = kbuf[c * CHUNK:c * CHUNK + nkeys, :].astype(BF16)
        vals = vbuf[c * CHUNK:c * CHUNK + nkeys, :].astype(BF16)
        for g in range(SWA_GROUP):
            qg = qn_swa[rows, g * LANES:(g + 1) * LANES]
            qstack = jnp.concatenate([jnp.where(lo_w, qg, 0.0), jnp.where(lo_w, 0.0, qg)], axis=0)
            s = _dot_nt(qstack.astype(BF16), keys) * SWA_SCALE
            if first:
                key_pos = j * tl + (c * CHUNK - WINDOW) + key_off
                s = jnp.where(key_pos >= 0, s, -jnp.inf)
            sink = jnp.where(top_rows, sink_ref[g], sink_ref[SWA_GROUP + g])
            mx = jnp.maximum(jnp.max(s, axis=-1, keepdims=True), sink)
            e = jnp.exp(s - mx)
            inv_den = 1.0 / (jnp.sum(e, axis=-1, keepdims=True) + jnp.exp(sink - mx))
            pv = _dot((e * inv_den).astype(BF16), vals)
            o = jnp.where(lo_w, pv[:CHUNK], pv[CHUNK:])
            col = GDN_OUT + MLP_WIDTH + g * LANES
            mix_ref[rows, col:col + LANES] = o.astype(mix_ref.dtype)

    for c in range(nchunk):
        for it in items[c * GDN_PAIRS:(c + 1) * GDN_PAIRS]:
            gdn_scan_step(it)
        swa_chunk(c)
        if ((c + 1) * CHUNK) % mc == 0:
            gmlp_chunk(((c + 1) * CHUNK) // mc - 1)
    for r, state in zip(pair_rows, states):
        s_scr[r, :] = state

    k_tail = kbuf[tl:tl + WINDOW, :]
    v_tail = vbuf[tl:tl + WINDOW, :]
    if tl >= WINDOW:
        kbuf[0:WINDOW, :] = k_tail
        vbuf[0:WINDOW, :] = v_tail

    @pl.when(j == nj - 1)
    def _final():
        for h in range(GDN_HEADS):
            s_out[h] = s_scr[h * GDN_DK:(h + 1) * GDN_DK, :]
        conv_out[...] = conv_tail
        k_out[...] = k_tail
        v_out[...] = v_tail


def _mixers(first, proj3d, lw, init, tl):
    b, l, _ = proj3d.shape
    mc = MLP_CHUNK if first else CHUNK
    nj = l // tl
    full = lambda shape: pl.BlockSpec(shape, lambda bi, ji: (0,) * len(shape))
    per_b = lambda shape: pl.BlockSpec((None,) + shape, lambda bi, ji: (bi,) + (0,) * len(shape))
    in_specs = [pl.BlockSpec((None, tl, PROJ_COLS), lambda bi, ji: (bi, ji, 0)),
                full((GDN_CONV, GDN_QKV)), full((2, LANES)), full((1, GDN_DV)),
                full((1, MLP_WIDTH)), full((1, MLP_WIDTH)), full((MLP_HEADS, mc, mc)), full((mc, MLP_WIDTH)),
                full((1, SWA_Q)), full((1, SWA_KV)),
                pl.BlockSpec(memory_space=pltpu.SMEM)]
    args = [proj3d, lw["conv_w"], lw["gvec"], lw["gdn_norm_g"], lw["ln_g"], lw["ln_b"],
            lw["ws"] if first else lw["ws"][:, :mc, :mc], lw["bs_tile"][:mc], lw["q_g"], lw["k_g"], lw["sinks"]]
    out_specs = [pl.BlockSpec((None, tl, MIX_WIDTH), lambda bi, ji: (bi, ji, 0)),
                 per_b((GDN_HEADS, GDN_DK, GDN_DV)), per_b((CONV_PAD, GDN_QKV)),
                 per_b((WINDOW, SWA_KV)), per_b((WINDOW, SWA_KV))]
    out_shape = [jax.ShapeDtypeStruct((b, l, MIX_WIDTH), BF16),
                 jax.ShapeDtypeStruct((b, GDN_HEADS, GDN_DK, GDN_DV), F32),
                 jax.ShapeDtypeStruct((b, CONV_PAD, GDN_QKV), F32),
                 jax.ShapeDtypeStruct((b, WINDOW, SWA_KV), F32),
                 jax.ShapeDtypeStruct((b, WINDOW, SWA_KV), F32)]
    if not first:
        conv0, s0, k0, v0 = init
        in_specs += [per_b((CONV_PAD, GDN_QKV)), per_b((GDN_HEADS, GDN_DK, GDN_DV)),
                     per_b((WINDOW, SWA_KV)), per_b((WINDOW, SWA_KV))]
        args += [conv0, s0, k0, v0]
        out_specs.append(pl.BlockSpec((None, tl, MLP_WIDTH), lambda bi, ji: (bi, ji, 0)))
        out_shape.append(jax.ShapeDtypeStruct((b, l, MLP_WIDTH), F32))
    return pl.pallas_call(
        functools.partial(_mixer_kernel, first, tl, mc),
        grid=(b, nj),
        in_specs=in_specs,
        out_specs=out_specs,
        out_shape=out_shape,
        scratch_shapes=[pltpu.VMEM((GDN_HEADS * GDN_DK, GDN_DV), F32),
                        pltpu.VMEM((CONV_PAD + tl, GDN_QKV), F32),
                        pltpu.VMEM((WINDOW + tl, SWA_KV), F32),
                        pltpu.VMEM((WINDOW + tl, SWA_KV), F32)],
        compiler_params=pltpu.CompilerParams(dimension_semantics=("arbitrary", "arbitrary"),
                                             vmem_limit_bytes=VMEM_LIMIT),
        name="mixers_prompt" if first else "mixers_sample",
    )(*args)


def _layer_weights(l, w_in, conv_w, a_log, dt_bias, gdn_norm_g, ln_g, ln_b, ws, bs, q_g, k_g, sinks,
                   w_out, norm1_g, norm2_g, w_gate, w_up, w_down):
    d = w_in.shape[1]
    offs = np.cumsum((0,) + IN_SIZES)
    cols = [w_in[l][:, offs[i]:offs[i + 1]] for i in range(len(IN_SIZES))]
    qkv, z, b_raw, a_raw, u, vm, sq, sk, sv = cols
    order = np.asarray(SWA_HEAD_ORDER)
    sq = sq.reshape(d, SWA_HEADS, SWA_HEAD_DIM)[:, order].reshape(d, SWA_Q)
    ba = jnp.concatenate([b_raw, a_raw, jnp.zeros((d, LANES - 2 * GDN_HEADS), F32)], axis=1)
    w_in_r = jnp.concatenate([qkv, z, u, vm, sq, sk, sv, ba], axis=1).astype(BF16)
    wo = w_out[l]
    wo_swa = wo[GDN_OUT + MLP_WIDTH:].reshape(SWA_HEADS, SWA_HEAD_DIM, d)[order].reshape(SWA_Q, d)
    wo_r = jnp.concatenate([wo[:GDN_OUT + MLP_WIDTH], wo_swa], axis=0).astype(BF16)
    pad = jnp.zeros((GDN_HEADS,), F32)
    lane_vec = lambda v: jnp.concatenate([pad, v, jnp.zeros((LANES - 2 * GDN_HEADS,), F32)])
    return dict(
        w_in=w_in_r, w_out=wo_r,
        norm1_g=norm1_g[l][None, :], norm2_g=norm2_g[l][None, :],
        conv_w=conv_w[l],
        gvec=jnp.stack([lane_vec(a_log[l]), lane_vec(dt_bias[l])]),
        gdn_norm_g=gdn_norm_g[l][None, :],
        ln_g=ln_g[l].reshape(1, MLP_WIDTH), ln_b=ln_b[l].reshape(1, MLP_WIDTH),
        ws=ws[l],
        bs_tile=jnp.repeat(bs[l].T, MLP_HEAD_DIM, axis=1),
        q_g=jnp.tile(q_g[l], SWA_HEADS)[None, :], k_g=jnp.tile(k_g[l], SWA_KV_HEADS)[None, :],
        sinks=sinks[l],
        w_gate=w_gate[l].astype(BF16), w_up=w_up[l].astype(BF16), w_down=w_down[l].astype(BF16),
    )


def _layer(x, first, lw, init, tm, tl):
    b, l, d = x.shape
    x2d = x.reshape(b * l, d)
    proj = _in_proj(x2d, lw["norm1_g"], lw["w_in"], tm).reshape(b, l, PROJ_COLS)
    outs = _mixers(first, proj, lw, init, tl)
    mix = outs[0].reshape(b * l, MIX_WIDTH)
    y = _out_ffn(mix, x2d, lw["w_out"], lw["norm2_g"], lw["w_gate"], lw["w_up"], lw["w_down"], tm)
    return y.reshape(b, l, d), outs[1:]


def kernel(x_prompt, x_sample, cache_swa_k, cache_swa_v, state_gdn, state_gdn_conv, norm1_g, w_in, gdn_conv_w, gdn_a_log, gdn_dt_bias, gdn_norm_g, mlp_ln_g, mlp_ln_b, mlp_ws, mlp_bs, swa_q_norm_g, swa_k_norm_g, swa_sinks, w_out, norm2_g, ffn_w_gate, ffn_w_up, ffn_w_down):
    depth = w_in.shape[0]
    bs_dec, dec_seq = x_sample.shape[:2]
    bp = x_prompt.shape[0]
    yp, ys = x_prompt, x_sample
    p_k, p_v, p_s, p_c = [], [], [], []
    s_k, s_v, s_s, s_c, s_m = [], [], [], [], []
    kv_shape = (WINDOW, SWA_KV_HEADS, SWA_HEAD_DIM)
    for l in range(depth):
        lw = _layer_weights(l, w_in, gdn_conv_w, gdn_a_log, gdn_dt_bias, gdn_norm_g, mlp_ln_g, mlp_ln_b,
                            mlp_ws, mlp_bs, swa_q_norm_g, swa_k_norm_g, swa_sinks, w_out, norm1_g, norm2_g,
                            ffn_w_gate, ffn_w_up, ffn_w_down)
        yp, (st, cv, k_, v_) = _layer(yp, True, lw, None, 512, 256)
        p_k.append(k_.reshape((bp,) + kv_shape))
        p_v.append(v_.reshape((bp,) + kv_shape))
        p_s.append(st)
        p_c.append(cv[:, CONV_PAD - (GDN_CONV - 1):])
        conv0 = jnp.pad(state_gdn_conv[l], ((0, 0), (CONV_PAD - (GDN_CONV - 1), 0), (0, 0)))
        init = (conv0, state_gdn[l], cache_swa_k[l].reshape(bs_dec, WINDOW, SWA_KV),
                cache_swa_v[l].reshape(bs_dec, WINDOW, SWA_KV))
        ys, (st, cv, k_, v_, vr) = _layer(ys, False, lw, init, 512, dec_seq)
        s_k.append(k_.reshape((bs_dec,) + kv_shape))
        s_v.append(v_.reshape((bs_dec,) + kv_shape))
        s_s.append(st)
        s_c.append(cv[:, CONV_PAD - (GDN_CONV - 1):])
        s_m.append(vr.reshape(bs_dec, dec_seq, MLP_HEADS, MLP_HEAD_DIM))
    return (yp, ys, jnp.stack(p_k), jnp.stack(p_v), jnp.stack(p_s), jnp.stack(p_c),
            jnp.stack(s_k), jnp.stack(s_v), jnp.stack(s_s), jnp.stack(s_c), jnp.stack(s_m))
```

```python
import functools

import numpy as np
import jax
import jax.numpy as jnp
from jax import lax
from jax.experimental import pallas as pl
from jax.experimental.pallas import tpu as pltpu

F32 = jnp.float32
BF16 = jnp.bfloat16
HIGHEST = lax.Precision.HIGHEST

LANES = 128
CHUNK = 64
EPS = 1e-6
GDN_HEADS, GDN_DK, GDN_DV, GDN_CONV = 4, 128, 128, 4
GDN_PAIRS = GDN_HEADS // 2
GDN_QKV = GDN_HEADS * (2 * GDN_DK + GDN_DV)
MLP_HEADS, MLP_HEAD_DIM, MLP_CHUNK = 4, 64, 128
MLP_WIDTH = MLP_HEADS * MLP_HEAD_DIM
SWA_HEADS, SWA_KV_HEADS, SWA_HEAD_DIM, WINDOW = 4, 2, 64, 128
SWA_GROUP = SWA_HEADS // SWA_KV_HEADS
SWA_SCALE = SWA_HEAD_DIM ** -0.5
SWA_Q = SWA_HEADS * SWA_HEAD_DIM
SWA_KV = SWA_KV_HEADS * SWA_HEAD_DIM
GDN_OUT = GDN_HEADS * GDN_DV
MIX_WIDTH = GDN_OUT + MLP_WIDTH + SWA_Q
IN_SIZES = (GDN_QKV, GDN_OUT, GDN_HEADS, GDN_HEADS, MLP_WIDTH, MLP_WIDTH, SWA_Q, SWA_KV, SWA_KV)

C_QKV = 0
C_Z = C_QKV + GDN_QKV
C_U = C_Z + GDN_OUT
C_VM = C_U + MLP_WIDTH
C_SQ = C_VM + MLP_WIDTH
C_SK = C_SQ + SWA_Q
C_SV = C_SK + SWA_KV
C_BA = C_SV + SWA_KV
PROJ_COLS = C_BA + LANES
CONV_PAD = 8
SWA_HEAD_ORDER = (0, 2, 1, 3)

VMEM_LIMIT = 56 * 1024 * 1024

FFN_COLS = 256
FFN_PLAN = (1, 1, 1, 1,  1, 1, 1, 1,  1, 1, 1, 1, 1, 1,  1, 0, 0,  0, 0, 0, 0, 0, 0, 0, 0)


def _require(ok):
    assert ok, "out-projection pieces must run before the mixers overwrite the mix buffer"


def _dot(a, b):
    return jnp.dot(a, b, preferred_element_type=F32)


def _dot_nt(a, b):
    return lax.dot_general(a, b, (((1,), (1,)), ((), ())), preferred_element_type=F32)


def _dot_tn(a, b):
    return lax.dot_general(a, b, (((0,), (0,)), ((), ())), preferred_element_type=F32)


def _dot_exact(a, b):
    return jnp.dot(a, b, precision=HIGHEST, preferred_element_type=F32)


def _dot_split(x, m):
    hi = x.astype(BF16)
    lo = (x - hi.astype(F32)).astype(BF16)
    return _dot(hi, m) + _dot(lo, m)


def _dot_split3(x, m):
    hi = x.astype(BF16)
    r = x - hi.astype(F32)
    mid = r.astype(BF16)
    lo = (r - mid.astype(F32)).astype(BF16)
    return _dot(hi, m) + _dot(mid, m) + _dot(lo, m)


def _lane_spread_matrix(first_lane, groups):
    r = lax.broadcasted_iota(jnp.int32, (LANES, groups * LANES), 0)
    c = lax.broadcasted_iota(jnp.int32, (LANES, groups * LANES), 1)
    return jnp.where(r == first_lane + c // LANES, 1.0, 0.0).astype(BF16)


def _silu(x):
    return x * jax.nn.sigmoid(x)


def _softplus(x):
    return jnp.maximum(x, 0.0) + jnp.log1p(jnp.exp(-jnp.abs(x)))


def _group_mean_matrix(width, group):
    r = lax.broadcasted_iota(jnp.int32, (width, width), 0) // group
    c = lax.broadcasted_iota(jnp.int32, (width, width), 1) // group
    return jnp.where(r == c, 1.0 / group, 0.0).astype(BF16)


def _in_proj_kernel(x_ref, g_ref, w_ref, o_ref):
    x = x_ref[...]
    h = x * lax.rsqrt(jnp.mean(x * x, axis=-1, keepdims=True) + EPS) * g_ref[...]
    o_ref[...] = _dot(h.astype(BF16), w_ref[...])


def _in_proj(x2d, g, w, tm):
    t, d = x2d.shape
    n = w.shape[1]
    return pl.pallas_call(
        _in_proj_kernel,
        grid=(t // tm,),
        in_specs=[pl.BlockSpec((tm, d), lambda i: (i, 0)),
                  pl.BlockSpec((1, d), lambda i: (0, 0)),
                  pl.BlockSpec((d, n), lambda i: (0, 0))],
        out_specs=pl.BlockSpec((tm, n), lambda i: (i, 0)),
        out_shape=jax.ShapeDtypeStruct((t, n), F32),
        compiler_params=pltpu.CompilerParams(dimension_semantics=("arbitrary",),
                                             vmem_limit_bytes=VMEM_LIMIT),
        name="in_proj",
    )(x2d, g, w)


def _out_ffn_kernel(mix_ref, x_ref, wo_ref, g_ref, wg_ref, wu_ref, wd_ref, y_ref):
    x1 = x_ref[...] + _dot(mix_ref[...], wo_ref[...])
    h = (x1 * lax.rsqrt(jnp.mean(x1 * x1, axis=-1, keepdims=True) + EPS) * g_ref[...]).astype(BF16)
    act = _silu(_dot(h, wg_ref[...])) * _dot(h, wu_ref[...])
    y_ref[...] = x1 + _dot(act.astype(BF16), wd_ref[...])


def _out_ffn(mix2d, x2d, wo, g2, wg, wu, wd, tm):
    t, d = x2d.shape
    dff = wg.shape[1]
    const = lambda i: (0, 0)
    resident = dict(pipeline_mode=pl.Buffered(1))
    return pl.pallas_call(
        _out_ffn_kernel,
        grid=(t // tm,),
        in_specs=[pl.BlockSpec((tm, d), lambda i: (i, 0)),
                  pl.BlockSpec((tm, d), lambda i: (i, 0)),
                  pl.BlockSpec((d, d), const, **resident),
                  pl.BlockSpec((1, d), const),
                  pl.BlockSpec((d, dff), const, **resident),
                  pl.BlockSpec((d, dff), const, **resident),
                  pl.BlockSpec((dff, d), const, **resident)],
        out_specs=pl.BlockSpec((tm, d), lambda i: (i, 0)),
        out_shape=jax.ShapeDtypeStruct((t, d), F32),
        compiler_params=pltpu.CompilerParams(dimension_semantics=("arbitrary",),
                                             vmem_limit_bytes=VMEM_LIMIT),
        name="out_ffn",
    )(mix2d, x2d, wo, g2, wg, wu, wd)


def _block_diag(x, lo_lanes):
    return jnp.concatenate([jnp.where(lo_lanes, x, 0.0), jnp.where(lo_lanes, 0.0, x)], axis=0).astype(BF16)


def _unit_lower_inverse_wide(a_list, eye_w, lo_lanes, fill):
    ps = [-a for a in a_list]
    ts = [eye_w + p for p in ps]
    ps = [_dot(p.astype(BF16), _block_diag(p, lo_lanes)) for p in ps]
    fill()
    for _ in range(int(np.log2(CHUNK)) - 2):
        ys = [_dot(jnp.concatenate([t, p], axis=0).astype(BF16), _block_diag(p, lo_lanes)) for t, p in zip(ts, ps)]
        ts = [t + y[:CHUNK] for t, y in zip(ts, ys)]
        ps = [y[CHUNK:] for y in ys]
        fill()
    ts = [t + _dot(t.astype(BF16), _block_diag(p, lo_lanes)) for t, p in zip(ts, ps)]
    fill()
    return ts


N_MIXER_PARAMS = 10


def _mixer_kernel(first, tl, mc, *refs):
    proj_ref, params, rest = refs[0], refs[1:1 + N_MIXER_PARAMS], refs[1 + N_MIXER_PARAMS:]
    if first:
        init, vrow_out = None, None
        mix_ref, s_out, conv_out, k_out, v_out = rest[:5]
    else:
        init = rest[:4]
        mix_ref, s_out, conv_out, k_out, v_out, vrow_out = rest[4:10]
    j = pl.program_id(1)
    _mixer_init(j, init, rest[-4:])
    _mixer_compute(first, tl, mc, j, j == pl.num_programs(1) - 1, proj_ref, params,
                   mix_ref, (s_out, conv_out, k_out, v_out), vrow_out, rest[-4:])


def _mix_ffn_kernel(tl, mc, nj, nt, *refs):
    proj_ref, x_ref = refs[:2]
    params = refs[2:2 + N_MIXER_PARAMS]
    wo_ref, g2_ref, wg_ref, wu_ref, wd_ref = refs[2 + N_MIXER_PARAMS:7 + N_MIXER_PARAMS]
    y_ref, s_out, conv_out, k_out, v_out = refs[7 + N_MIXER_PARAMS:12 + N_MIXER_PARAMS]
    s_scr, xbuf, kbuf, vbuf, mix_scr, h_scr = refs[12 + N_MIXER_PARAMS:]
    t = pl.program_id(0)

    scratch = (s_scr, xbuf, kbuf, vbuf)
    j = lax.rem(jnp.minimum(t, nt - 1), nj)

    @pl.when(t == 0)
    def _first_step():
        mix_scr[...] = jnp.zeros_like(mix_scr)

    _mixer_init(j, None, scratch)

    d = x_ref.shape[1]
    dff = wg_ref.shape[1]
    pieces = []

    def out_proj_piece(c0):
        def run():
            cols = slice(c0, c0 + FFN_COLS)
            y_ref[:, cols] = x_ref[:, cols] + _dot(mix_scr[...], wo_ref[:, cols])
        return run

    def ffn_piece(c0, with_norm):
        def run():
            if with_norm:
                x1 = y_ref[...]
                h_scr[...] = (x1 * lax.rsqrt(jnp.mean(x1 * x1, axis=-1, keepdims=True) + EPS)
                              * g2_ref[...]).astype(BF16)
            cols = slice(c0, c0 + FFN_COLS)
            h = h_scr[...]
            act = _silu(_dot(h, wg_ref[:, cols])) * _dot(h, wu_ref[:, cols])
            y_ref[...] += _dot(act.astype(BF16), wd_ref[cols, :])
        return run

    pieces += [out_proj_piece(c0) for c0 in range(0, d, FFN_COLS)]
    pieces += [ffn_piece(c0, c0 == 0) for c0 in range(0, dff, FFN_COLS)]
    n_out_proj = d // FFN_COLS
    plan = list(FFN_PLAN)
    done = [0]

    def filler():
        for _ in range(plan.pop(0) if plan else 0):
            if pieces:
                pieces.pop(0)()
                done[0] += 1

    _mixer_compute(True, tl, mc, j, (j == nj - 1) & (t < nt), proj_ref, params,
                   mix_scr, (s_out, conv_out, k_out, v_out), None, scratch, filler,
                   before_mix_store=lambda: _require(done[0] >= n_out_proj))
    while pieces:
        pieces.pop(0)()


def _mixer_init(j, init, scratch):
    s_scr, xbuf, kbuf, vbuf = scratch

    @pl.when(j == 0)
    def _init():
        if init is None:
            s_scr[...] = jnp.zeros_like(s_scr)
            xbuf[0:CONV_PAD, :] = jnp.zeros((CONV_PAD, GDN_QKV), F32)
            kbuf[0:WINDOW, :] = jnp.zeros((WINDOW, SWA_KV), F32)
            vbuf[0:WINDOW, :] = jnp.zeros((WINDOW, SWA_KV), F32)
        else:
            conv0_ref, s0_ref, k0_ref, v0_ref = init
            for h in range(GDN_HEADS):
                s_scr[h * GDN_DK:(h + 1) * GDN_DK, :] = s0_ref[h]
            xbuf[0:CONV_PAD, :] = conv0_ref[...]
            kbuf[0:WINDOW, :] = k0_ref[...]
            vbuf[0:WINDOW, :] = v0_ref[...]


def _mixer_compute(first, tl, mc, j, is_last, proj_ref, params, mix_ref, state_outs, vrow_out, scratch, filler=None,
                   before_mix_store=None):
    convw_ref, gvec_ref, normg_ref, lng_ref, lnb_ref, ws_ref, bsb_ref, qg_ref, kg_ref, sink_ref = params
    s_out, conv_out, k_out, v_out = state_outs
    s_scr, xbuf, kbuf, vbuf = scratch
    nchunk = tl // CHUNK

    fill = filler if filler is not None else (lambda n=1: None)

    xbuf[CONV_PAD:CONV_PAD + tl, :] = proj_ref[:, C_QKV:C_QKV + GDN_QKV]
    base = CONV_PAD - (GDN_CONV - 1)

    def conv_silu(col):
        cols = slice(col, col + LANES)
        acc = xbuf[base:base + tl, cols] * convw_ref[0:1, cols]
        for w in range(1, GDN_CONV):
            acc = acc + xbuf[base + w:base + w + tl, cols] * convw_ref[w:w + 1, cols]
        return _silu(acc)

    qn, kn, vv = [], [], []
    for h in range(GDN_HEADS):
        q_h = conv_silu(h * GDN_DK)
        k_h = conv_silu(GDN_OUT + h * GDN_DK)
        qn.append(q_h * lax.rsqrt(jnp.sum(q_h * q_h, axis=-1, keepdims=True) + EPS) * (GDN_DK ** -0.5))
        kn.append(k_h * lax.rsqrt(jnp.sum(k_h * k_h, axis=-1, keepdims=True) + EPS))
        vv.append(conv_silu(2 * GDN_OUT + h * GDN_DV))
        fill()
    conv_tail = xbuf[tl:tl + CONV_PAD, :]
    xbuf[0:CONV_PAD, :] = conv_tail

    ba = proj_ref[:, C_BA:C_BA + LANES]
    beta_all = jax.nn.sigmoid(ba)
    g_all = -jnp.exp(gvec_ref[0:1, :]) * _softplus(ba + gvec_ref[1:2, :])
    r_i = lax.broadcasted_iota(jnp.int32, (tl, tl), 0)
    c_i = lax.broadcasted_iota(jnp.int32, (tl, tl), 1)
    cum_mat = jnp.where((r_i // CHUNK == c_i // CHUNK) & (c_i <= r_i), 1.0, 0.0).astype(F32)
    gc_all = _dot_exact(cum_mat, g_all)
    beta_b = _dot_split3(beta_all, _lane_spread_matrix(0, GDN_HEADS))
    gc_b = _dot_split3(gc_all, _lane_spread_matrix(GDN_HEADS, GDN_HEADS))

    row_w = lax.broadcasted_iota(jnp.int32, (CHUNK, LANES), 0)
    lane_w = lax.broadcasted_iota(jnp.int32, (CHUNK, LANES), 1)
    lane_in = lane_w & (CHUNK - 1)
    lo_w = lane_w < CHUNK
    incl_w = lane_in <= row_w
    strict_w = lane_in < row_w
    eye_w = jnp.where(lane_in == row_w, 1.0, 0.0).astype(F32)
    top_st = lax.broadcasted_iota(jnp.int32, (2 * CHUNK, LANES), 0) < CHUNK
    head0_rows = (lax.broadcasted_iota(jnp.int32, (4 * CHUNK, LANES), 0) & CHUNK) == 0
    zero_blk = jnp.zeros((CHUNK, GDN_DK), F32)
    ones8 = jnp.ones((8, LANES), F32)

    items = []
    for c in range(nchunk):
        rows = slice(c * CHUNK, (c + 1) * CHUNK)
        gc_c = gc_all[rows]
        g_heads = jnp.concatenate([jnp.where(lane_w == GDN_HEADS + h, gc_c, 0.0) for h in range(GDN_HEADS)], axis=0)
        grow_all = lax.dot_general(ones8, g_heads, (((1,), (1,)), ((), ())), precision=HIGHEST,
                                   preferred_element_type=F32)
        for p in range(GDN_PAIRS):
            h0, h1 = 2 * p, 2 * p + 1
            q0, q1, k0, k1, v0, v1 = qn[h0][rows], qn[h1][rows], kn[h0][rows], kn[h1][rows], vv[h0][rows], vv[h1][rows]
            b0, b1 = beta_b[rows, h0 * LANES:(h0 + 1) * LANES], beta_b[rows, h1 * LANES:(h1 + 1) * LANES]
            g0, g1 = gc_b[rows, h0 * LANES:(h0 + 1) * LANES], gc_b[rows, h1 * LANES:(h1 + 1) * LANES]
            kb0, kb1 = k0 * b0, k1 * b1
            lhs = jnp.concatenate([jnp.concatenate([kb0, kb1], axis=-1),
                                   jnp.concatenate([q0, q1], axis=-1)], axis=0).astype(BF16)
            rhs = jnp.concatenate([jnp.concatenate([k0, zero_blk], axis=-1),
                                   jnp.concatenate([zero_blk, k1], axis=-1)], axis=0).astype(BF16)
            x = _dot_nt(lhs, rhs)
            grow_w = grow_all[0:1, p * LANES:(p + 1) * LANES]
            decay = jnp.exp(jnp.where(incl_w, jnp.where(lo_w, g0, g1) - grow_w, -jnp.inf))
            e0, e1 = jnp.exp(g0), jnp.exp(g1)
            gl0, gl1 = g0[CHUNK - 1:CHUNK, :], g1[CHUNK - 1:CHUNK, :]
            items.append(dict(
                c=c, p=p,
                a=jnp.where(strict_w, x[:CHUNK] * decay, 0.0),
                attn=x[CHUNK:] * decay,
                rhs=jnp.concatenate([jnp.concatenate([v0 * b0, kb0 * e0], axis=-1),
                                     jnp.concatenate([v1 * b1, kb1 * e1], axis=-1)], axis=0).astype(BF16),
                qg=jnp.concatenate([q0 * e0, q1 * e1], axis=0),
                kd=jnp.concatenate([k0 * jnp.exp(gl0 - g0), k1 * jnp.exp(gl1 - g1)], axis=0),
                scale=jnp.concatenate([jnp.broadcast_to(jnp.exp(gl0), (GDN_DK, GDN_DV)),
                                       jnp.broadcast_to(jnp.exp(gl1), (GDN_DK, GDN_DV))], axis=0)))
        fill()
    tinvs = _unit_lower_inverse_wide([it["a"] for it in items], eye_w, lo_w, fill)
    for it, tinv in zip(items, tinvs):
        it["sol"] = _dot(_block_diag(tinv, lo_w), it["rhs"])
    fill()

    pair_rows = [slice(p * 2 * GDN_DK, (p + 1) * 2 * GDN_DK) for p in range(GDN_PAIRS)]
    states = [s_scr[r, :] for r in pair_rows]

    def gdn_scan_read(it):
        wq = jnp.concatenate([it["sol"][:, GDN_DV:], it["qg"]], axis=0)
        lhs = jnp.concatenate([jnp.where(head0_rows, wq, 0.0), jnp.where(head0_rows, 0.0, wq)], axis=-1)
        return _dot(lhs.astype(BF16), states[it["p"]].astype(BF16))

    def gdn_scan_update(it, y):
        p, rows = it["p"], slice(it["c"] * CHUNK, (it["c"] + 1) * CHUNK)
        state = states[p]
        v_new = (it["sol"][:, :GDN_DV] - y[:2 * CHUNK]).astype(BF16)
        o = y[2 * CHUNK:] + _dot(_block_diag(it["attn"], lo_w), v_new)
        kd_w = jnp.concatenate([jnp.where(top_st, it["kd"], 0.0), jnp.where(top_st, 0.0, it["kd"])], axis=-1)
        states[p] = state * it["scale"] + _dot_tn(kd_w.astype(BF16), v_new)
        o = o * lax.rsqrt(jnp.mean(o * o, axis=-1, keepdims=True) + EPS) * normg_ref[...]
        for i in range(2):
            h = 2 * p + i
            z = proj_ref[rows, C_Z + h * GDN_DV:C_Z + (h + 1) * GDN_DV]
            mix_ref[rows, h * GDN_DV:(h + 1) * GDN_DV] = (o[i * CHUNK:(i + 1) * CHUNK] * _silu(z)).astype(mix_ref.dtype)

    mean_mlp = _group_mean_matrix(MLP_WIDTH, MLP_HEAD_DIM)
    u_act = jax.nn.gelu(proj_ref[:, C_U:C_U + MLP_WIDTH])
    v_act = jax.nn.gelu(proj_ref[:, C_VM:C_VM + MLP_WIDTH])
    v_cent = v_act - _dot_split(v_act, mean_mlp)
    v_var = _dot_split(v_cent * v_cent, mean_mlp)
    v_norm = v_cent * lax.rsqrt(v_var + EPS) * lng_ref[...] + lnb_ref[...]
    if not first:
        vrow_out[...] = v_norm
    fill()
    mi = lax.broadcasted_iota(jnp.int32, (mc, mc), 0)
    mj = lax.broadcasted_iota(jnp.int32, (mc, mc), 1)
    block_causal = (mi // CHUNK) >= (mj // CHUNK)
    low_half = lax.broadcasted_iota(jnp.int32, (mc, LANES), 1) < MLP_HEAD_DIM

    def gmlp_chunk(m):
        rows = slice(m * mc, (m + 1) * mc)
        for p in range(MLP_WIDTH // LANES):
            lanes = slice(p * LANES, (p + 1) * LANES)
            vpair = v_norm[rows, lanes].astype(BF16)
            w0 = jnp.where(block_causal, ws_ref[2 * p], 0.0).astype(BF16)
            w1 = jnp.where(block_causal, ws_ref[2 * p + 1], 0.0).astype(BF16)
            s = jnp.where(low_half, _dot(w0, vpair), _dot(w1, vpair)) + bsb_ref[:, lanes]
            mix_ref[rows, GDN_OUT + p * LANES:GDN_OUT + (p + 1) * LANES] = (u_act[rows, lanes] * s).astype(mix_ref.dtype)

    mean_head = _group_mean_matrix(SWA_Q, SWA_HEAD_DIM)
    sq = proj_ref[:, C_SQ:C_SQ + SWA_Q]
    sk = proj_ref[:, C_SK:C_SK + SWA_KV]
    qn_swa = sq * lax.rsqrt(_dot_split(sq * sq, mean_head) + EPS) * qg_ref[...]
    kn_swa = sk * lax.rsqrt(_dot_split(sk * sk, mean_head[:SWA_KV, :SWA_KV]) + EPS) * kg_ref[...]
    kbuf[WINDOW:WINDOW + tl, :] = kn_swa
    vbuf[WINDOW:WINDOW + tl, :] = proj_ref[:, C_SV:C_SV + SWA_KV]
    fill()
    nkeys = WINDOW + CHUNK
    top_rows = lax.broadcasted_iota(jnp.int32, (2 * CHUNK, 1), 0) < CHUNK
    key_off = lax.broadcasted_iota(jnp.int32, (2 * CHUNK, nkeys), 1)

    def swa_scores(c):
        rows = slice(c * CHUNK, (c + 1) * CHUNK)
        keys = kbuf[c * CHUNK:c * CHUNK + nkeys, :].astype(BF16)
        out = []
        for g in range(SWA_GROUP):
            qg = qn_swa[rows, g * LANES:(g + 1) * LANES]
            qstack = jnp.concatenate([jnp.where(lo_w, qg, 0.0), jnp.where(lo_w, 0.0, qg)], axis=0)
            out.append(_dot_nt(qstack.astype(BF16), keys))
        return out

    def swa_attend(c, scores):
        rows = slice(c * CHUNK, (c + 1) * CHUNK)
        vals = vbuf[c * CHUNK:c * CHUNK + nkeys, :].astype(BF16)
        for g in range(SWA_GROUP):
            s = scores[g] * SWA_SCALE
            if first:
                key_pos = j * tl + (c * CHUNK - WINDOW) + key_off
                s = jnp.where(key_pos >= 0, s, -jnp.inf)
            sink = jnp.where(top_rows, sink_ref[g], sink_ref[SWA_GROUP + g])
            mx = jnp.maximum(jnp.max(s, axis=-1, keepdims=True), sink)
            e = jnp.exp(s - mx)
            inv_den = 1.0 / (jnp.sum(e, axis=-1, keepdims=True) + jnp.exp(sink - mx))
            pv = _dot((e * inv_den).astype(BF16), vals)
            o = jnp.where(lo_w, pv[:CHUNK], pv[CHUNK:])
            col = GDN_OUT + MLP_WIDTH + g * LANES
            mix_ref[rows, col:col + LANES] = o.astype(mix_ref.dtype)

    if before_mix_store is not None:
        before_mix_store()
    for c in range(nchunk):
        chunk_items = items[c * GDN_PAIRS:(c + 1) * GDN_PAIRS]
        ys = [gdn_scan_read(it) for it in chunk_items]
        scores = swa_scores(c)
        fill()
        for it, y in zip(chunk_items, ys):
            gdn_scan_update(it, y)
        swa_attend(c, scores)
        if ((c + 1) * CHUNK) % mc == 0:
            gmlp_chunk(((c + 1) * CHUNK) // mc - 1)
        fill()
    for r, state in zip(pair_rows, states):
        s_scr[r, :] = state

    k_tail = kbuf[tl:tl + WINDOW, :]
    v_tail = vbuf[tl:tl + WINDOW, :]
    if tl >= WINDOW:
        kbuf[0:WINDOW, :] = k_tail
        vbuf[0:WINDOW, :] = v_tail

    @pl.when(is_last)
    def _final():
        for h in range(GDN_HEADS):
            s_out[h] = s_scr[h * GDN_DK:(h + 1) * GDN_DK, :]
        conv_out[...] = conv_tail
        k_out[...] = k_tail
        v_out[...] = v_tail


def _mixers(first, proj3d, lw, init, tl):
    b, l, _ = proj3d.shape
    mc = MLP_CHUNK if first else CHUNK
    nj = l // tl
    per_b = lambda shape: pl.BlockSpec((None,) + shape, lambda bi, ji: (bi,) + (0,) * len(shape))
    pspecs, pargs = _mixer_param_args(first, lw, mc)
    in_specs = [pl.BlockSpec((None, tl, PROJ_COLS), lambda bi, ji: (bi, ji, 0))] + pspecs
    args = [proj3d] + pargs
    out_specs = [pl.BlockSpec((None, tl, MIX_WIDTH), lambda bi, ji: (bi, ji, 0)),
                 per_b((GDN_HEADS, GDN_DK, GDN_DV)), per_b((CONV_PAD, GDN_QKV)),
                 per_b((WINDOW, SWA_KV)), per_b((WINDOW, SWA_KV))]
    out_shape = [jax.ShapeDtypeStruct((b, l, MIX_WIDTH), BF16),
                 jax.ShapeDtypeStruct((b, GDN_HEADS, GDN_DK, GDN_DV), F32),
                 jax.ShapeDtypeStruct((b, CONV_PAD, GDN_QKV), F32),
                 jax.ShapeDtypeStruct((b, WINDOW, SWA_KV), F32),
                 jax.ShapeDtypeStruct((b, WINDOW, SWA_KV), F32)]
    if not first:
        conv0, s0, k0, v0 = init
        in_specs += [per_b((CONV_PAD, GDN_QKV)), per_b((GDN_HEADS, GDN_DK, GDN_DV)),
                     per_b((WINDOW, SWA_KV)), per_b((WINDOW, SWA_KV))]
        args += [conv0, s0, k0, v0]
        out_specs.append(pl.BlockSpec((None, tl, MLP_WIDTH), lambda bi, ji: (bi, ji, 0)))
        out_shape.append(jax.ShapeDtypeStruct((b, l, MLP_WIDTH), F32))
    return pl.pallas_call(
        functools.partial(_mixer_kernel, first, tl, mc),
        grid=(b, nj),
        in_specs=in_specs,
        out_specs=out_specs,
        out_shape=out_shape,
        scratch_shapes=[pltpu.VMEM((GDN_HEADS * GDN_DK, GDN_DV), F32),
                        pltpu.VMEM((CONV_PAD + tl, GDN_QKV), F32),
                        pltpu.VMEM((WINDOW + tl, SWA_KV), F32),
                        pltpu.VMEM((WINDOW + tl, SWA_KV), F32)],
        compiler_params=pltpu.CompilerParams(dimension_semantics=("arbitrary", "arbitrary"),
                                             vmem_limit_bytes=VMEM_LIMIT),
        name="mixers_prompt" if first else "mixers_sample",
    )(*args)


def _mixer_param_args(first, lw, mc):
    full = lambda shape: pl.BlockSpec(shape, lambda *_: (0,) * len(shape))
    specs = [full((GDN_CONV, GDN_QKV)), full((2, LANES)), full((1, GDN_DV)),
             full((1, MLP_WIDTH)), full((1, MLP_WIDTH)), full((MLP_HEADS, mc, mc)), full((mc, MLP_WIDTH)),
             full((1, SWA_Q)), full((1, SWA_KV)),
             pl.BlockSpec(memory_space=pltpu.SMEM)]
    args = [lw["conv_w"], lw["gvec"], lw["gdn_norm_g"], lw["ln_g"], lw["ln_b"],
            lw["ws"] if first else lw["ws"][:, :mc, :mc], lw["bs_tile"][:mc], lw["q_g"], lw["k_g"], lw["sinks"]]
    return specs, args


def _mix_ffn_prompt(proj3d, x3d, lw, tl):
    b, l, d = x3d.shape
    dff = lw["w_gate"].shape[1]
    nj = l // tl
    nt = b * nj
    mix_tile = lambda t: (jnp.minimum(t, nt - 1) // nj, jnp.minimum(t, nt - 1) % nj, 0)
    ffn_tile = lambda t: (jnp.maximum(t - 1, 0) // nj, jnp.maximum(t - 1, 0) % nj, 0)
    per_b = lambda shape: pl.BlockSpec((None,) + shape, lambda t: (jnp.minimum(t, nt - 1) // nj,) + (0,) * len(shape))
    const = lambda t: (0, 0)
    resident = dict(pipeline_mode=pl.Buffered(1))
    pspecs, pargs = _mixer_param_args(True, lw, MLP_CHUNK)
    in_specs = ([pl.BlockSpec((None, tl, PROJ_COLS), mix_tile), pl.BlockSpec((None, tl, d), ffn_tile)] + pspecs
                + [pl.BlockSpec((d, d), const, **resident), pl.BlockSpec((1, d), const),
                   pl.BlockSpec((d, dff), const, **resident), pl.BlockSpec((d, dff), const, **resident),
                   pl.BlockSpec((dff, d), const, **resident)])
    args = [proj3d, x3d] + pargs + [lw["w_out"], lw["norm2_g"], lw["w_gate"], lw["w_up"], lw["w_down"]]
    out_specs = [pl.BlockSpec((None, tl, d), ffn_tile),
                 per_b((GDN_HEADS, GDN_DK, GDN_DV)), per_b((CONV_PAD, GDN_QKV)),
                 per_b((WINDOW, SWA_KV)), per_b((WINDOW, SWA_KV))]
    out_shape = [jax.ShapeDtypeStruct((b, l, d), F32),
                 jax.ShapeDtypeStruct((b, GDN_HEADS, GDN_DK, GDN_DV), F32),
                 jax.ShapeDtypeStruct((b, CONV_PAD, GDN_QKV), F32),
                 jax.ShapeDtypeStruct((b, WINDOW, SWA_KV), F32),
                 jax.ShapeDtypeStruct((b, WINDOW, SWA_KV), F32)]
    return pl.pallas_call(
        functools.partial(_mix_ffn_kernel, tl, MLP_CHUNK, nj, nt),
        grid=(nt + 1,),
        in_specs=in_specs,
        out_specs=out_specs,
        out_shape=out_shape,
        scratch_shapes=[pltpu.VMEM((GDN_HEADS * GDN_DK, GDN_DV), F32),
                        pltpu.VMEM((CONV_PAD + tl, GDN_QKV), F32),
                        pltpu.VMEM((WINDOW + tl, SWA_KV), F32),
                        pltpu.VMEM((WINDOW + tl, SWA_KV), F32),
                        pltpu.VMEM((tl, MIX_WIDTH), BF16),
                        pltpu.VMEM((tl, d), BF16)],
        compiler_params=pltpu.CompilerParams(dimension_semantics=("arbitrary",),
                                             vmem_limit_bytes=VMEM_LIMIT),
        name="mix_ffn_prompt",
    )(*args)


def _layer_weights(l, w_in, conv_w, a_log, dt_bias, gdn_norm_g, ln_g, ln_b, ws, bs, q_g, k_g, sinks,
                   w_out, norm1_g, norm2_g, w_gate, w_up, w_down):
    d = w_in.shape[1]
    offs = np.cumsum((0,) + IN_SIZES)
    cols = [w_in[l][:, offs[i]:offs[i + 1]] for i in range(len(IN_SIZES))]
    qkv, z, b_raw, a_raw, u, vm, sq, sk, sv = cols
    order = np.asarray(SWA_HEAD_ORDER)
    sq = sq.reshape(d, SWA_HEADS, SWA_HEAD_DIM)[:, order].reshape(d, SWA_Q)
    ba = jnp.concatenate([b_raw, a_raw, jnp.zeros((d, LANES - 2 * GDN_HEADS), F32)], axis=1)
    w_in_r = jnp.concatenate([qkv, z, u, vm, sq, sk, sv, ba], axis=1).astype(BF16)
    wo = w_out[l]
    wo_swa = wo[GDN_OUT + MLP_WIDTH:].reshape(SWA_HEADS, SWA_HEAD_DIM, d)[order].reshape(SWA_Q, d)
    wo_r = jnp.concatenate([wo[:GDN_OUT + MLP_WIDTH], wo_swa], axis=0).astype(BF16)
    pad = jnp.zeros((GDN_HEADS,), F32)
    lane_vec = lambda v: jnp.concatenate([pad, v, jnp.zeros((LANES - 2 * GDN_HEADS,), F32)])
    return dict(
        w_in=w_in_r, w_out=wo_r,
        norm1_g=norm1_g[l][None, :], norm2_g=norm2_g[l][None, :],
        conv_w=conv_w[l],
        gvec=jnp.stack([lane_vec(a_log[l]), lane_vec(dt_bias[l])]),
        gdn_norm_g=gdn_norm_g[l][None, :],
        ln_g=ln_g[l].reshape(1, MLP_WIDTH), ln_b=ln_b[l].reshape(1, MLP_WIDTH),
        ws=ws[l],
        bs_tile=jnp.repeat(bs[l].T, MLP_HEAD_DIM, axis=1),
        q_g=jnp.tile(q_g[l], SWA_HEADS)[None, :], k_g=jnp.tile(k_g[l], SWA_KV_HEADS)[None, :],
        sinks=sinks[l],
        w_gate=w_gate[l].astype(BF16), w_up=w_up[l].astype(BF16), w_down=w_down[l].astype(BF16),
    )


def _layer(x, first, lw, init, tm, tl):
    b, l, d = x.shape
    x2d = x.reshape(b * l, d)
    proj = _in_proj(x2d, lw["norm1_g"], lw["w_in"], tm).reshape(b, l, PROJ_COLS)
    if first:
        outs = _mix_ffn_prompt(proj, x, lw, tl)
        return outs[0], outs[1:]
    outs = _mixers(first, proj, lw, init, tl)
    mix = outs[0].reshape(b * l, MIX_WIDTH)
    y = _out_ffn(mix, x2d, lw["w_out"], lw["norm2_g"], lw["w_gate"], lw["w_up"], lw["w_down"], tm)
    return y.reshape(b, l, d), outs[1:]


def kernel(x_prompt, x_sample, cache_swa_k, cache_swa_v, state_gdn, state_gdn_conv, norm1_g, w_in, gdn_conv_w, gdn_a_log, gdn_dt_bias, gdn_norm_g, mlp_ln_g, mlp_ln_b, mlp_ws, mlp_bs, swa_q_norm_g, swa_k_norm_g, swa_sinks, w_out, norm2_g, ffn_w_gate, ffn_w_up, ffn_w_down):
    depth = w_in.shape[0]
    bs_dec, dec_seq = x_sample.shape[:2]
    bp = x_prompt.shape[0]
    yp, ys = x_prompt, x_sample
    p_k, p_v, p_s, p_c = [], [], [], []
    s_k, s_v, s_s, s_c, s_m = [], [], [], [], []
    kv_shape = (WINDOW, SWA_KV_HEADS, SWA_HEAD_DIM)
    for l in range(depth):
        lw = _layer_weights(l, w_in, gdn_conv_w, gdn_a_log, gdn_dt_bias, gdn_norm_g, mlp_ln_g, mlp_ln_b,
                            mlp_ws, mlp_bs, swa_q_norm_g, swa_k_norm_g, swa_sinks, w_out, norm1_g, norm2_g,
                            ffn_w_gate, ffn_w_up, ffn_w_down)
        yp, (st, cv, k_, v_) = _layer(yp, True, lw, None, 512, 256)
        p_k.append(k_.reshape((bp,) + kv_shape))
        p_v.append(v_.reshape((bp,) + kv_shape))
        p_s.append(st)
        p_c.append(cv[:, CONV_PAD - (GDN_CONV - 1):])
        conv0 = jnp.pad(state_gdn_conv[l], ((0, 0), (CONV_PAD - (GDN_CONV - 1), 0), (0, 0)))
        init = (conv0, state_gdn[l], cache_swa_k[l].reshape(bs_dec, WINDOW, SWA_KV),
                cache_swa_v[l].reshape(bs_dec, WINDOW, SWA_KV))
        ys, (st, cv, k_, v_, vr) = _layer(ys, False, lw, init, 512, dec_seq)
        s_k.append(k_.reshape((bs_dec,) + kv_shape))
        s_v.append(v_.reshape((bs_dec,) + kv_shape))
        s_s.append(st)
        s_c.append(cv[:, CONV_PAD - (GDN_CONV - 1):])
        s_m.append(vr.reshape(bs_dec, dec_seq, MLP_HEADS, MLP_HEAD_DIM))
    return (yp, ys, jnp.stack(p_k), jnp.stack(p_v), jnp.stack(p_s), jnp.stack(p_c),
            jnp.stack(s_k), jnp.stack(s_v), jnp.stack(s_s), jnp.stack(s_c), jnp.stack(s_m))
```

```python
import functools

import numpy as np
import jax
import jax.numpy as jnp
from jax import lax
from jax.experimental import pallas as pl
from jax.experimental.pallas import tpu as pltpu

F32 = jnp.float32
BF16 = jnp.bfloat16
HIGHEST = lax.Precision.HIGHEST

LANES = 128
CHUNK = 64
EPS = 1e-6
GDN_HEADS, GDN_DK, GDN_DV, GDN_CONV = 4, 128, 128, 4
GDN_PAIRS = GDN_HEADS // 2
GDN_QKV = GDN_HEADS * (2 * GDN_DK + GDN_DV)
MLP_HEADS, MLP_HEAD_DIM, MLP_CHUNK = 4, 64, 128
MLP_WIDTH = MLP_HEADS * MLP_HEAD_DIM
SWA_HEADS, SWA_KV_HEADS, SWA_HEAD_DIM, WINDOW = 4, 2, 64, 128
SWA_GROUP = SWA_HEADS // SWA_KV_HEADS
SWA_SCALE = SWA_HEAD_DIM ** -0.5
SWA_Q = SWA_HEADS * SWA_HEAD_DIM
SWA_KV = SWA_KV_HEADS * SWA_HEAD_DIM
GDN_OUT = GDN_HEADS * GDN_DV
MIX_WIDTH = GDN_OUT + MLP_WIDTH + SWA_Q
IN_SIZES = (GDN_QKV, GDN_OUT, GDN_HEADS, GDN_HEADS, MLP_WIDTH, MLP_WIDTH, SWA_Q, SWA_KV, SWA_KV)

C_QKV = 0
C_Z = C_QKV + GDN_QKV
C_U = C_Z + GDN_OUT
C_VM = C_U + MLP_WIDTH
C_SQ = C_VM + MLP_WIDTH
C_SK = C_SQ + SWA_Q
C_SV = C_SK + SWA_KV
C_BA = C_SV + SWA_KV
PROJ_COLS = C_BA + LANES
CONV_PAD = 8
SWA_HEAD_ORDER = (0, 2, 1, 3)

VMEM_LIMIT = 56 * 1024 * 1024
DENSE_TILE = 512
PROMPT_TILE = 256
PROMPT_STREAMS = 2
SAMPLE_STREAMS = 4


def _dot(a, b):
    return jnp.dot(a, b, preferred_element_type=F32)


def _dot_nt(a, b):
    return lax.dot_general(a, b, (((1,), (1,)), ((), ())), preferred_element_type=F32)


def _dot_tn(a, b):
    return lax.dot_general(a, b, (((0,), (0,)), ((), ())), preferred_element_type=F32)


def _dot_exact(a, b):
    return jnp.dot(a, b, precision=HIGHEST, preferred_element_type=F32)


def _dot_split(x, m):
    hi = x.astype(BF16)
    lo = (x - hi.astype(F32)).astype(BF16)
    return _dot(hi, m) + _dot(lo, m)


def _dot_split3(x, m):
    hi = x.astype(BF16)
    r = x - hi.astype(F32)
    mid = r.astype(BF16)
    lo = (r - mid.astype(F32)).astype(BF16)
    return _dot(hi, m) + _dot(mid, m) + _dot(lo, m)


def _lane_spread_matrix(first_lane, groups):
    r = lax.broadcasted_iota(jnp.int32, (LANES, groups * LANES), 0)
    c = lax.broadcasted_iota(jnp.int32, (LANES, groups * LANES), 1)
    return jnp.where(r == first_lane + c // LANES, 1.0, 0.0).astype(BF16)


def _sigmoid(x):
    return 0.5 * jnp.tanh(0.5 * x) + 0.5


def _silu(x):
    return x * _sigmoid(x)


def _softplus(x):
    return jnp.maximum(x, 0.0) + jnp.log1p(jnp.exp(-jnp.abs(x)))


def _group_mean_matrix(width, group):
    r = lax.broadcasted_iota(jnp.int32, (width, width), 0) // group
    c = lax.broadcasted_iota(jnp.int32, (width, width), 1) // group
    return jnp.where(r == c, 1.0 / group, 0.0).astype(BF16)


def _in_proj_kernel(x_ref, g_ref, w_ref, o_ref):
    x = x_ref[...]
    h = x * lax.rsqrt(jnp.mean(x * x, axis=-1, keepdims=True) + EPS) * g_ref[...]
    o_ref[...] = _dot(h.astype(BF16), w_ref[...])


def _in_proj(x2d, g, w, tm):
    t, d = x2d.shape
    n = w.shape[1]
    return pl.pallas_call(
        _in_proj_kernel,
        grid=(t // tm,),
        in_specs=[pl.BlockSpec((tm, d), lambda i: (i, 0)),
                  pl.BlockSpec((1, d), lambda i: (0, 0)),
                  pl.BlockSpec((d, n), lambda i: (0, 0))],
        out_specs=pl.BlockSpec((tm, n), lambda i: (i, 0)),
        out_shape=jax.ShapeDtypeStruct((t, n), F32),
        compiler_params=pltpu.CompilerParams(dimension_semantics=("arbitrary",),
                                             vmem_limit_bytes=VMEM_LIMIT),
        name="in_proj",
    )(x2d, g, w)


def _out_ffn_kernel(mix_ref, x_ref, wo_ref, g_ref, wg_ref, wu_ref, wd_ref, y_ref):
    x1 = x_ref[...] + _dot(mix_ref[...], wo_ref[...])
    h = (x1 * lax.rsqrt(jnp.mean(x1 * x1, axis=-1, keepdims=True) + EPS) * g_ref[...]).astype(BF16)
    act = _silu(_dot(h, wg_ref[...])) * _dot(h, wu_ref[...])
    y_ref[...] = x1 + _dot(act.astype(BF16), wd_ref[...])


def _out_ffn(mix2d, x2d, wo, g2, wg, wu, wd, tm):
    t, d = x2d.shape
    dff = wg.shape[1]
    const = lambda i: (0, 0)
    resident = dict(pipeline_mode=pl.Buffered(1))
    return pl.pallas_call(
        _out_ffn_kernel,
        grid=(t // tm,),
        in_specs=[pl.BlockSpec((tm, d), lambda i: (i, 0)),
                  pl.BlockSpec((tm, d), lambda i: (i, 0)),
                  pl.BlockSpec((d, d), const, **resident),
                  pl.BlockSpec((1, d), const),
                  pl.BlockSpec((d, dff), const, **resident),
                  pl.BlockSpec((d, dff), const, **resident),
                  pl.BlockSpec((dff, d), const, **resident)],
        out_specs=pl.BlockSpec((tm, d), lambda i: (i, 0)),
        out_shape=jax.ShapeDtypeStruct((t, d), F32),
        compiler_params=pltpu.CompilerParams(dimension_semantics=("arbitrary",),
                                             vmem_limit_bytes=VMEM_LIMIT),
        name="out_ffn",
    )(mix2d, x2d, wo, g2, wg, wu, wd)


def _block_diag(x, lo_lanes):
    return jnp.concatenate([jnp.where(lo_lanes, x, 0.0), jnp.where(lo_lanes, 0.0, x)], axis=0).astype(BF16)


def _unit_lower_inverse_wide(a_list, eye_w, lo_lanes):
    ps = [-a for a in a_list]
    ts = [eye_w + p for p in ps]
    ps = [_dot(p.astype(BF16), _block_diag(p, lo_lanes)) for p in ps]
    for _ in range(int(np.log2(CHUNK)) - 2):
        ys = [_dot(jnp.concatenate([t, p], axis=0).astype(BF16), _block_diag(p, lo_lanes)) for t, p in zip(ts, ps)]
        ts = [t + y[:CHUNK] for t, y in zip(ts, ys)]
        ps = [y[CHUNK:] for y in ys]
    return [t + _dot(t.astype(BF16), _block_diag(p, lo_lanes)) for t, p in zip(ts, ps)]


N_MIXER_PARAMS = 10


def _mixer_kernel(first, streams, tl, mc, *refs):
    proj_ref = refs[0]
    convw_ref, gvec_ref, normg_ref, lng_ref, lnb_ref, ws_ref, bsb_ref, qg_ref, kg_ref, sink_ref = refs[1:1 + N_MIXER_PARAMS]
    rest = refs[1 + N_MIXER_PARAMS:]
    if first:
        mix_ref, s_out, conv_out, k_out, v_out = rest[:5]
    else:
        conv0_ref, s0_ref, k0_ref, v0_ref = rest[:4]
        mix_ref, s_out, conv_out, k_out, v_out, vrow_out = rest[4:10]
    s_scr, xbuf, kbuf, vbuf = rest[-4:]
    j = pl.program_id(1)
    nchunk = tl // CHUNK
    seqs = range(streams)

    @pl.when(j == 0)
    def _init():
        if first:
            s_scr[...] = jnp.zeros_like(s_scr)
            xbuf[:, 0:CONV_PAD, :] = jnp.zeros((streams, CONV_PAD, GDN_QKV), F32)
            kbuf[:, 0:WINDOW, :] = jnp.zeros((streams, WINDOW, SWA_KV), F32)
            vbuf[:, 0:WINDOW, :] = jnp.zeros((streams, WINDOW, SWA_KV), F32)
        else:
            for g in seqs:
                for h in range(GDN_HEADS):
                    s_scr[g, h * GDN_DK:(h + 1) * GDN_DK, :] = s0_ref[g, h]
            xbuf[:, 0:CONV_PAD, :] = conv0_ref[...]
            kbuf[:, 0:WINDOW, :] = k0_ref[...]
            vbuf[:, 0:WINDOW, :] = v0_ref[...]

    row_w = lax.broadcasted_iota(jnp.int32, (CHUNK, LANES), 0)
    lane_w = lax.broadcasted_iota(jnp.int32, (CHUNK, LANES), 1)
    lane_in = lane_w & (CHUNK - 1)
    lo_w = lane_w < CHUNK
    incl_w = lane_in <= row_w
    strict_w = lane_in < row_w
    eye_w = jnp.where(lane_in == row_w, 1.0, 0.0).astype(F32)
    top_st = lax.broadcasted_iota(jnp.int32, (2 * CHUNK, LANES), 0) < CHUNK
    head0_rows = (lax.broadcasted_iota(jnp.int32, (4 * CHUNK, LANES), 0) & CHUNK) == 0
    zero_blk = jnp.zeros((CHUNK, GDN_DK), F32)
    ones8 = jnp.ones((8, LANES), F32)
    r_i = lax.broadcasted_iota(jnp.int32, (tl, tl), 0)
    c_i = lax.broadcasted_iota(jnp.int32, (tl, tl), 1)
    cum_mat = jnp.where((r_i // CHUNK == c_i // CHUNK) & (c_i <= r_i), 1.0, 0.0).astype(F32)
    spread_beta = _lane_spread_matrix(0, GDN_HEADS)
    spread_gc = _lane_spread_matrix(GDN_HEADS, GDN_HEADS)
    mean_mlp = _group_mean_matrix(MLP_WIDTH, MLP_HEAD_DIM)

    qn, kn, vv, beta_b, gc_b, gc_all, conv_tail, u_act, v_norm, qn_swa = ({} for _ in range(10))
    for g in seqs:
        xbuf[g, CONV_PAD:CONV_PAD + tl, :] = proj_ref[g, :, C_QKV:C_QKV + GDN_QKV]

        def conv_silu(col):
            cols = slice(col, col + LANES)
            x = xbuf[g, :, cols]
            x1 = pltpu.roll(x, 1, axis=0)
            near = x * convw_ref[3:4, cols] + x1 * convw_ref[2:3, cols]
            far = x * convw_ref[1:2, cols] + x1 * convw_ref[0:1, cols]
            return _silu((near + pltpu.roll(far, 2, axis=0))[CONV_PAD:])

        for h in range(GDN_HEADS):
            q_h = conv_silu(h * GDN_DK)
            k_h = conv_silu(GDN_OUT + h * GDN_DK)
            qn[g, h] = q_h * lax.rsqrt(jnp.sum(q_h * q_h, axis=-1, keepdims=True) + EPS) * (GDN_DK ** -0.5)
            kn[g, h] = k_h * lax.rsqrt(jnp.sum(k_h * k_h, axis=-1, keepdims=True) + EPS)
            vv[g, h] = conv_silu(2 * GDN_OUT + h * GDN_DV)
        conv_tail[g] = xbuf[g, tl:tl + CONV_PAD, :]
        xbuf[g, 0:CONV_PAD, :] = conv_tail[g]

        ba = proj_ref[g, :, C_BA:C_BA + LANES]
        beta_all = _sigmoid(ba)
        g_all = -jnp.exp(gvec_ref[0:1, :]) * _softplus(ba + gvec_ref[1:2, :])
        gc_all[g] = _dot_exact(cum_mat, g_all)
        beta_b[g] = _dot_split3(beta_all, spread_beta)
        gc_b[g] = _dot_split3(gc_all[g], spread_gc)

        u_act[g] = jax.nn.gelu(proj_ref[g, :, C_U:C_U + MLP_WIDTH])
        v_act = jax.nn.gelu(proj_ref[g, :, C_VM:C_VM + MLP_WIDTH])
        v_cent = v_act - _dot_split(v_act, mean_mlp)
        v_var = _dot_split(v_cent * v_cent, mean_mlp)
        v_norm[g] = v_cent * lax.rsqrt(v_var + EPS) * lng_ref[...] + lnb_ref[...]
        if not first:
            vrow_out[g] = v_norm[g]

        sq = proj_ref[g, :, C_SQ:C_SQ + SWA_Q]
        sk = proj_ref[g, :, C_SK:C_SK + SWA_KV]
        qn_swa[g] = sq * lax.rsqrt(_dot_split(sq * sq, mean_mlp) + EPS) * qg_ref[...]
        kbuf[g, WINDOW:WINDOW + tl, :] = (sk * lax.rsqrt(_dot_split(sk * sk, mean_mlp[:SWA_KV, :SWA_KV]) + EPS)
                                          * kg_ref[...])
        vbuf[g, WINDOW:WINDOW + tl, :] = proj_ref[g, :, C_SV:C_SV + SWA_KV]

    items = []
    for c in range(nchunk):
        rows = slice(c * CHUNK, (c + 1) * CHUNK)
        for g in seqs:
            gc_c = gc_all[g][rows]
            g_heads = jnp.concatenate([jnp.where(lane_w == GDN_HEADS + h, gc_c, 0.0) for h in range(GDN_HEADS)], axis=0)
            grow_all = lax.dot_general(ones8, g_heads, (((1,), (1,)), ((), ())), precision=HIGHEST,
                                       preferred_element_type=F32)
            for p in range(GDN_PAIRS):
                h0, h1 = 2 * p, 2 * p + 1
                q0, q1, k0, k1 = qn[g, h0][rows], qn[g, h1][rows], kn[g, h0][rows], kn[g, h1][rows]
                v0, v1 = vv[g, h0][rows], vv[g, h1][rows]
                b0, b1 = beta_b[g][rows, h0 * LANES:(h0 + 1) * LANES], beta_b[g][rows, h1 * LANES:(h1 + 1) * LANES]
                g0, g1 = gc_b[g][rows, h0 * LANES:(h0 + 1) * LANES], gc_b[g][rows, h1 * LANES:(h1 + 1) * LANES]
                kb0, kb1 = k0 * b0, k1 * b1
                lhs = jnp.concatenate([jnp.concatenate([kb0, kb1], axis=-1),
                                       jnp.concatenate([q0, q1], axis=-1)], axis=0).astype(BF16)
                rhs = jnp.concatenate([jnp.concatenate([k0, zero_blk], axis=-1),
                                       jnp.concatenate([zero_blk, k1], axis=-1)], axis=0).astype(BF16)
                x = _dot_nt(lhs, rhs)
                grow_w = grow_all[0:1, p * LANES:(p + 1) * LANES]
                decay = jnp.exp(jnp.where(incl_w, jnp.where(lo_w, g0, g1) - grow_w, -jnp.inf))
                e0, e1 = jnp.exp(g0), jnp.exp(g1)
                gl0, gl1 = g0[CHUNK - 1:CHUNK, :], g1[CHUNK - 1:CHUNK, :]
                items.append(dict(
                    c=c, g=g, p=p,
                    a=jnp.where(strict_w, x[:CHUNK] * decay, 0.0),
                    attn=x[CHUNK:] * decay,
                    rhs=jnp.concatenate([jnp.concatenate([v0 * b0, kb0 * e0], axis=-1),
                                         jnp.concatenate([v1 * b1, kb1 * e1], axis=-1)], axis=0).astype(BF16),
                    qg=jnp.concatenate([q0 * e0, q1 * e1], axis=0),
                    kd=jnp.concatenate([k0 * jnp.exp(gl0 - g0), k1 * jnp.exp(gl1 - g1)], axis=0),
                    scale=jnp.concatenate([jnp.broadcast_to(jnp.exp(gl0), (GDN_DK, GDN_DV)),
                                           jnp.broadcast_to(jnp.exp(gl1), (GDN_DK, GDN_DV))], axis=0)))
    tinvs = _unit_lower_inverse_wide([it["a"] for it in items], eye_w, lo_w)
    for it, tinv in zip(items, tinvs):
        it["sol"] = _dot(_block_diag(tinv, lo_w), it["rhs"])

    pair_rows = [slice(p * 2 * GDN_DK, (p + 1) * 2 * GDN_DK) for p in range(GDN_PAIRS)]
    states = {(g, p): s_scr[g, pair_rows[p], :] for g in seqs for p in range(GDN_PAIRS)}

    def gdn_scan_read(it):
        wq = jnp.concatenate([it["sol"][:, GDN_DV:], it["qg"]], axis=0)
        lhs = jnp.concatenate([jnp.where(head0_rows, wq, 0.0), jnp.where(head0_rows, 0.0, wq)], axis=-1)
        return _dot(lhs.astype(BF16), states[it["g"], it["p"]].astype(BF16))

    def gdn_scan_update(it, y):
        g, p, rows = it["g"], it["p"], slice(it["c"] * CHUNK, (it["c"] + 1) * CHUNK)
        v_new = (it["sol"][:, :GDN_DV] - y[:2 * CHUNK]).astype(BF16)
        o = y[2 * CHUNK:] + _dot(_block_diag(it["attn"], lo_w), v_new)
        kd_w = jnp.concatenate([jnp.where(top_st, it["kd"], 0.0), jnp.where(top_st, 0.0, it["kd"])], axis=-1)
        states[g, p] = states[g, p] * it["scale"] + _dot_tn(kd_w.astype(BF16), v_new)
        o = o * lax.rsqrt(jnp.mean(o * o, axis=-1, keepdims=True) + EPS) * normg_ref[...]
        for i in range(2):
            h = 2 * p + i
            z = proj_ref[g, rows, C_Z + h * GDN_DV:C_Z + (h + 1) * GDN_DV]
            mix_ref[g, rows, h * GDN_DV:(h + 1) * GDN_DV] = (o[i * CHUNK:(i + 1) * CHUNK] * _silu(z)).astype(mix_ref.dtype)

    mi = lax.broadcasted_iota(jnp.int32, (mc, mc), 0)
    mj = lax.broadcasted_iota(jnp.int32, (mc, mc), 1)
    block_causal = (mi // CHUNK) >= (mj // CHUNK)
    low_half = lax.broadcasted_iota(jnp.int32, (mc, LANES), 1) < MLP_HEAD_DIM
    ws_causal = [jnp.where(block_causal, ws_ref[h], 0.0).astype(BF16) for h in range(MLP_HEADS)]

    def gmlp_chunk(g, m):
        rows = slice(m * mc, (m + 1) * mc)
        for p in range(MLP_WIDTH // LANES):
            lanes = slice(p * LANES, (p + 1) * LANES)
            vpair = v_norm[g][rows, lanes].astype(BF16)
            s = jnp.where(low_half, _dot(ws_causal[2 * p], vpair), _dot(ws_causal[2 * p + 1], vpair)) + bsb_ref[:, lanes]
            mix_ref[g, rows, GDN_OUT + p * LANES:GDN_OUT + (p + 1) * LANES] = (u_act[g][rows, lanes] * s).astype(mix_ref.dtype)

    nkeys = WINDOW + CHUNK
    top_rows = lax.broadcasted_iota(jnp.int32, (2 * CHUNK, 1), 0) < CHUNK
    key_off = lax.broadcasted_iota(jnp.int32, (2 * CHUNK, nkeys), 1)

    def swa_scores(g, c):
        rows = slice(c * CHUNK, (c + 1) * CHUNK)
        keys = kbuf[g, c * CHUNK:c * CHUNK + nkeys, :].astype(BF16)
        out = []
        for grp in range(SWA_GROUP):
            qg = qn_swa[g][rows, grp * LANES:(grp + 1) * LANES]
            qstack = jnp.concatenate([jnp.where(lo_w, qg, 0.0), jnp.where(lo_w, 0.0, qg)], axis=0)
            out.append(_dot_nt(qstack.astype(BF16), keys))
        return out

    def swa_attend(g, c, scores):
        rows = slice(c * CHUNK, (c + 1) * CHUNK)
        vals = vbuf[g, c * CHUNK:c * CHUNK + nkeys, :].astype(BF16)
        for grp in range(SWA_GROUP):
            s = scores[grp] * SWA_SCALE
            if first:
                key_pos = j * tl + (c * CHUNK - WINDOW) + key_off
                s = jnp.where(key_pos >= 0, s, -jnp.inf)
            sink = jnp.where(top_rows, sink_ref[grp], sink_ref[SWA_GROUP + grp])
            mx = jnp.maximum(jnp.max(s, axis=-1, keepdims=True), sink)
            e = jnp.exp(s - mx)
            inv_den = 1.0 / (jnp.sum(e, axis=-1, keepdims=True) + jnp.exp(sink - mx))
            pv = _dot((e * inv_den).astype(BF16), vals)
            o = jnp.where(lo_w, pv[:CHUNK], pv[CHUNK:])
            col = GDN_OUT + MLP_WIDTH + grp * LANES
            mix_ref[g, rows, col:col + LANES] = o.astype(mix_ref.dtype)

    per_chunk = streams * GDN_PAIRS
    for c in range(nchunk):
        chunk_items = items[c * per_chunk:(c + 1) * per_chunk]
        ys = [gdn_scan_read(it) for it in chunk_items]
        scores = [swa_scores(g, c) for g in seqs]
        for it, y in zip(chunk_items, ys):
            gdn_scan_update(it, y)
        for g in seqs:
            swa_attend(g, c, scores[g])
            if ((c + 1) * CHUNK) % mc == 0:
                gmlp_chunk(g, ((c + 1) * CHUNK) // mc - 1)
    for (g, p), state in states.items():
        s_scr[g, pair_rows[p], :] = state

    k_tail = kbuf[:, tl:tl + WINDOW, :]
    v_tail = vbuf[:, tl:tl + WINDOW, :]
    if tl >= WINDOW:
        kbuf[:, 0:WINDOW, :] = k_tail
        vbuf[:, 0:WINDOW, :] = v_tail

    @pl.when(j == pl.num_programs(1) - 1)
    def _final():
        for g in seqs:
            for h in range(GDN_HEADS):
                s_out[g, h] = s_scr[g, h * GDN_DK:(h + 1) * GDN_DK, :]
            conv_out[g] = conv_tail[g]
        k_out[...] = k_tail
        v_out[...] = v_tail


def _mixers(first, proj3d, lw, init, tl, streams):
    b, l, _ = proj3d.shape
    mc = MLP_CHUNK if first else CHUNK
    full = lambda shape: pl.BlockSpec(shape, lambda bi, ji: (0,) * len(shape))
    per_b = lambda shape: pl.BlockSpec((streams,) + shape, lambda bi, ji: (bi,) + (0,) * len(shape))
    tile = lambda width: pl.BlockSpec((streams, tl, width), lambda bi, ji: (bi, ji, 0))
    state_specs = [per_b((GDN_HEADS, GDN_DK, GDN_DV)), per_b((CONV_PAD, GDN_QKV)),
                   per_b((WINDOW, SWA_KV)), per_b((WINDOW, SWA_KV))]
    in_specs = [tile(PROJ_COLS),
                full((GDN_CONV, GDN_QKV)), full((2, LANES)), full((1, GDN_DV)),
                full((1, MLP_WIDTH)), full((1, MLP_WIDTH)), full((MLP_HEADS, mc, mc)), full((mc, MLP_WIDTH)),
                full((1, SWA_Q)), full((1, SWA_KV)),
                pl.BlockSpec(memory_space=pltpu.SMEM)]
    args = [proj3d, lw["conv_w"], lw["gvec"], lw["gdn_norm_g"], lw["ln_g"], lw["ln_b"],
            lw["ws"] if first else lw["ws"][:, :mc, :mc], lw["bs_tile"][:mc], lw["q_g"], lw["k_g"], lw["sinks"]]
    out_specs = [tile(MIX_WIDTH)] + state_specs
    out_shape = [jax.ShapeDtypeStruct((b, l, MIX_WIDTH), BF16),
                 jax.ShapeDtypeStruct((b, GDN_HEADS, GDN_DK, GDN_DV), F32),
                 jax.ShapeDtypeStruct((b, CONV_PAD, GDN_QKV), F32),
                 jax.ShapeDtypeStruct((b, WINDOW, SWA_KV), F32),
                 jax.ShapeDtypeStruct((b, WINDOW, SWA_KV), F32)]
    if not first:
        conv0, s0, k0, v0 = init
        in_specs += [state_specs[1], state_specs[0], state_specs[2], state_specs[3]]
        args += [conv0, s0, k0, v0]
        out_specs.append(tile(MLP_WIDTH))
        out_shape.append(jax.ShapeDtypeStruct((b, l, MLP_WIDTH), F32))
    return pl.pallas_call(
        functools.partial(_mixer_kernel, first, streams, tl, mc),
        grid=(b // streams, l // tl),
        in_specs=in_specs,
        out_specs=out_specs,
        out_shape=out_shape,
        scratch_shapes=[pltpu.VMEM((streams, GDN_HEADS * GDN_DK, GDN_DV), F32),
                        pltpu.VMEM((streams, CONV_PAD + tl, GDN_QKV), F32),
                        pltpu.VMEM((streams, WINDOW + tl, SWA_KV), F32),
                        pltpu.VMEM((streams, WINDOW + tl, SWA_KV), F32)],
        compiler_params=pltpu.CompilerParams(dimension_semantics=("arbitrary", "arbitrary"),
                                             vmem_limit_bytes=VMEM_LIMIT),
        name="mixers_prompt" if first else "mixers_sample",
    )(*args)


def _layer_weights(l, w_in, conv_w, a_log, dt_bias, gdn_norm_g, ln_g, ln_b, ws, bs, q_g, k_g, sinks,
                   w_out, norm1_g, norm2_g, w_gate, w_up, w_down):
    d = w_in.shape[1]
    offs = np.cumsum((0,) + IN_SIZES)
    cols = [w_in[l][:, offs[i]:offs[i + 1]] for i in range(len(IN_SIZES))]
    qkv, z, b_raw, a_raw, u, vm, sq, sk, sv = cols
    order = np.asarray(SWA_HEAD_ORDER)
    sq = sq.reshape(d, SWA_HEADS, SWA_HEAD_DIM)[:, order].reshape(d, SWA_Q)
    ba = jnp.concatenate([b_raw, a_raw, jnp.zeros((d, LANES - 2 * GDN_HEADS), F32)], axis=1)
    w_in_r = jnp.concatenate([qkv, z, u, vm, sq, sk, sv, ba], axis=1).astype(BF16)
    wo = w_out[l]
    wo_swa = wo[GDN_OUT + MLP_WIDTH:].reshape(SWA_HEADS, SWA_HEAD_DIM, d)[order].reshape(SWA_Q, d)
    wo_r = jnp.concatenate([wo[:GDN_OUT + MLP_WIDTH], wo_swa], axis=0).astype(BF16)
    pad = jnp.zeros((GDN_HEADS,), F32)
    lane_vec = lambda v: jnp.concatenate([pad, v, jnp.zeros((LANES - 2 * GDN_HEADS,), F32)])
    return dict(
        w_in=w_in_r, w_out=wo_r,
        norm1_g=norm1_g[l][None, :], norm2_g=norm2_g[l][None, :],
        conv_w=conv_w[l],
        gvec=jnp.stack([lane_vec(a_log[l]), lane_vec(dt_bias[l])]),
        gdn_norm_g=gdn_norm_g[l][None, :],
        ln_g=ln_g[l].reshape(1, MLP_WIDTH), ln_b=ln_b[l].reshape(1, MLP_WIDTH),
        ws=ws[l],
        bs_tile=jnp.repeat(bs[l].T, MLP_HEAD_DIM, axis=1),
        q_g=jnp.tile(q_g[l], SWA_HEADS)[None, :], k_g=jnp.tile(k_g[l], SWA_KV_HEADS)[None, :],
        sinks=sinks[l],
        w_gate=w_gate[l].astype(BF16), w_up=w_up[l].astype(BF16), w_down=w_down[l].astype(BF16),
    )


def _layer(x, first, lw, init, tl, streams):
    b, l, d = x.shape
    x2d = x.reshape(b * l, d)
    proj = _in_proj(x2d, lw["norm1_g"], lw["w_in"], DENSE_TILE).reshape(b, l, PROJ_COLS)
    outs = _mixers(first, proj, lw, init, tl, streams)
    mix = outs[0].reshape(b * l, MIX_WIDTH)
    y = _out_ffn(mix, x2d, lw["w_out"], lw["norm2_g"], lw["w_gate"], lw["w_up"], lw["w_down"], DENSE_TILE)
    return y.reshape(b, l, d), outs[1:]


def kernel(x_prompt, x_sample, cache_swa_k, cache_swa_v, state_gdn, state_gdn_conv, norm1_g, w_in, gdn_conv_w, gdn_a_log, gdn_dt_bias, gdn_norm_g, mlp_ln_g, mlp_ln_b, mlp_ws, mlp_bs, swa_q_norm_g, swa_k_norm_g, swa_sinks, w_out, norm2_g, ffn_w_gate, ffn_w_up, ffn_w_down):
    depth = w_in.shape[0]
    bs_dec, dec_seq = x_sample.shape[:2]
    bp = x_prompt.shape[0]
    yp, ys = x_prompt, x_sample
    p_k, p_v, p_s, p_c = [], [], [], []
    s_k, s_v, s_s, s_c, s_m = [], [], [], [], []
    kv_shape = (WINDOW, SWA_KV_HEADS, SWA_HEAD_DIM)
    conv_rows = slice(CONV_PAD - (GDN_CONV - 1), CONV_PAD)
    for l in range(depth):
        lw = _layer_weights(l, w_in, gdn_conv_w, gdn_a_log, gdn_dt_bias, gdn_norm_g, mlp_ln_g, mlp_ln_b,
                            mlp_ws, mlp_bs, swa_q_norm_g, swa_k_norm_g, swa_sinks, w_out, norm1_g, norm2_g,
                            ffn_w_gate, ffn_w_up, ffn_w_down)
        yp, (st, cv, k_, v_) = _layer(yp, True, lw, None, PROMPT_TILE, PROMPT_STREAMS)
        p_k.append(k_.reshape((bp,) + kv_shape))
        p_v.append(v_.reshape((bp,) + kv_shape))
        p_s.append(st)
        p_c.append(cv[:, conv_rows])
        conv0 = jnp.pad(state_gdn_conv[l], ((0, 0), (CONV_PAD - (GDN_CONV - 1), 0), (0, 0)))
        init = (conv0, state_gdn[l], cache_swa_k[l].reshape(bs_dec, WINDOW, SWA_KV),
                cache_swa_v[l].reshape(bs_dec, WINDOW, SWA_KV))
        ys, (st, cv, k_, v_, vr) = _layer(ys, False, lw, init, dec_seq, SAMPLE_STREAMS)
        s_k.append(k_.reshape((bs_dec,) + kv_shape))
        s_v.append(v_.reshape((bs_dec,) + kv_shape))
        s_s.append(st)
        s_c.append(cv[:, conv_rows])
        s_m.append(vr.reshape(bs_dec, dec_seq, MLP_HEADS, MLP_HEAD_DIM))
    return (yp, ys, jnp.stack(p_k), jnp.stack(p_v), jnp.stack(p_s), jnp.stack(p_c),
            jnp.stack(s_k), jnp.stack(s_v), jnp.stack(s_s), jnp.stack(s_c), jnp.stack(s_m))
```

```python
import functools

import numpy as np
import jax
import jax.numpy as jnp
from jax import lax
from jax.experimental import pallas as pl
from jax.experimental.pallas import tpu as pltpu

F32 = jnp.float32
BF16 = jnp.bfloat16

LANES = 128
CHUNK = 64
EPS = 1e-6
GDN_HEADS, GDN_DK, GDN_DV, GDN_CONV = 4, 128, 128, 4
GDN_PAIRS = GDN_HEADS // 2
GDN_QKV = GDN_HEADS * (2 * GDN_DK + GDN_DV)
MLP_HEADS, MLP_HEAD_DIM, MLP_CHUNK = 4, 64, 128
MLP_WIDTH = MLP_HEADS * MLP_HEAD_DIM
SWA_HEADS, SWA_KV_HEADS, SWA_HEAD_DIM, WINDOW = 4, 2, 64, 128
SWA_GROUP = SWA_HEADS // SWA_KV_HEADS
SWA_SCALE = SWA_HEAD_DIM ** -0.5
SWA_Q = SWA_HEADS * SWA_HEAD_DIM
SWA_KV = SWA_KV_HEADS * SWA_HEAD_DIM
GDN_OUT = GDN_HEADS * GDN_DV
MIX_WIDTH = GDN_OUT + MLP_WIDTH + SWA_Q
IN_SIZES = (GDN_QKV, GDN_OUT, GDN_HEADS, GDN_HEADS, MLP_WIDTH, MLP_WIDTH, SWA_Q, SWA_KV, SWA_KV)

C_QKV = 0
C_Z = C_QKV + GDN_QKV
C_U = C_Z + GDN_OUT
C_VM = C_U + MLP_WIDTH
C_SQ = C_VM + MLP_WIDTH
C_SK = C_SQ + SWA_Q
C_SV = C_SK + SWA_KV
C_BA = C_SV + SWA_KV
PROJ_COLS = C_BA + LANES
CONV_PAD = 8
SWA_HEAD_ORDER = (0, 2, 1, 3)

VMEM_LIMIT = 56 * 1024 * 1024
DENSE_TILE = 512
PROMPT_TILE = 256
PROMPT_STREAMS = 2
SAMPLE_STREAMS = 4


def _dot(a, b):
    return jnp.dot(a, b, preferred_element_type=F32)


def _dot_nt(a, b):
    return lax.dot_general(a, b, (((1,), (1,)), ((), ())), preferred_element_type=F32)


def _dot_tn(a, b):
    return lax.dot_general(a, b, (((0,), (0,)), ((), ())), preferred_element_type=F32)


def _dot_split(x, m):
    hi = x.astype(BF16)
    lo = (x - hi.astype(F32)).astype(BF16)
    return _dot(hi, m) + _dot(lo, m)


def _split3(x):
    hi = x.astype(BF16)
    r = x - hi.astype(F32)
    mid = r.astype(BF16)
    return hi, mid, (r - mid.astype(F32)).astype(BF16)


def _dot_split3(x, m):
    hi, mid, lo = _split3(x)
    return _dot(hi, m) + _dot(mid, m) + _dot(lo, m)


def _lane_spread_matrix(first_lane, groups):
    r = lax.broadcasted_iota(jnp.int32, (LANES, groups * LANES), 0)
    c = lax.broadcasted_iota(jnp.int32, (LANES, groups * LANES), 1)
    return jnp.where(r == first_lane + c // LANES, 1.0, 0.0).astype(BF16)


def _sigmoid(x):
    return 0.5 * jnp.tanh(0.5 * x) + 0.5


def _silu(x):
    hx = 0.5 * x
    return hx + hx * jnp.tanh(hx)


def _softplus(x):
    return jnp.maximum(x, 0.0) + jnp.log1p(jnp.exp(-jnp.abs(x)))


def _group_mean_matrix(width, group):
    r = lax.broadcasted_iota(jnp.int32, (width, width), 0) // group
    c = lax.broadcasted_iota(jnp.int32, (width, width), 1) // group
    return jnp.where(r == c, 1.0 / group, 0.0).astype(BF16)


def _in_proj_kernel(x_ref, g_ref, w_ref, o_ref):
    x = x_ref[...]
    h = x * lax.rsqrt(jnp.mean(x * x, axis=-1, keepdims=True) + EPS) * g_ref[...]
    o_ref[...] = _dot(h.astype(BF16), w_ref[...])


def _layer_block(l, shape, **kwargs):
    return pl.BlockSpec((None,) + tuple(shape), lambda *_: (l,) + (0,) * len(shape), **kwargs)


def _in_proj(x2d, sw, l, tm):
    t, d = x2d.shape
    n = sw["w_in"].shape[2]
    return pl.pallas_call(
        _in_proj_kernel,
        grid=(t // tm,),
        in_specs=[pl.BlockSpec((tm, d), lambda i: (i, 0)), _layer_block(l, (1, d)), _layer_block(l, (d, n))],
        out_specs=pl.BlockSpec((tm, n), lambda i: (i, 0)),
        out_shape=jax.ShapeDtypeStruct((t, n), F32),
        compiler_params=pltpu.CompilerParams(dimension_semantics=("arbitrary",),
                                             vmem_limit_bytes=VMEM_LIMIT),
        name="in_proj",
    )(x2d, sw["norm1_g"], sw["w_in"])


def _out_ffn_kernel(mix_ref, x_ref, wo_ref, g_ref, wg_ref, wu_ref, wd_ref, y_ref):
    x1 = x_ref[...] + _dot(mix_ref[...], wo_ref[...])
    h = (x1 * lax.rsqrt(jnp.mean(x1 * x1, axis=-1, keepdims=True) + EPS) * g_ref[...]).astype(BF16)
    act = _silu(_dot(h, wg_ref[...])) * _dot(h, wu_ref[...])
    y_ref[...] = x1 + _dot(act.astype(BF16), wd_ref[...])


def _out_ffn(mix2d, x2d, sw, l, tm):
    t, d = x2d.shape
    dff = sw["w_gate"].shape[2]
    resident = dict(pipeline_mode=pl.Buffered(1))
    return pl.pallas_call(
        _out_ffn_kernel,
        grid=(t // tm,),
        in_specs=[pl.BlockSpec((tm, d), lambda i: (i, 0)),
                  pl.BlockSpec((tm, d), lambda i: (i, 0)),
                  _layer_block(l, (d, d), **resident),
                  _layer_block(l, (1, d)),
                  _layer_block(l, (d, dff), **resident),
                  _layer_block(l, (d, dff), **resident),
                  _layer_block(l, (dff, d), **resident)],
        out_specs=pl.BlockSpec((tm, d), lambda i: (i, 0)),
        out_shape=jax.ShapeDtypeStruct((t, d), F32),
        compiler_params=pltpu.CompilerParams(dimension_semantics=("arbitrary",),
                                             vmem_limit_bytes=VMEM_LIMIT),
        name="out_ffn",
    )(mix2d, x2d, sw["w_out"], sw["norm2_g"], sw["w_gate"], sw["w_up"], sw["w_down"])


def _block_diag(x, lo_lanes):
    return jnp.concatenate([jnp.where(lo_lanes, x, 0.0), jnp.where(lo_lanes, 0.0, x)], axis=0).astype(BF16)


def _unit_lower_inverse_wide(a_list, eye_w, lo_lanes):
    ps = [-a for a in a_list]
    ts = [eye_w + p for p in ps]
    ps = [_dot(p.astype(BF16), _block_diag(p, lo_lanes)) for p in ps]
    for _ in range(int(np.log2(CHUNK)) - 2):
        ys = [_dot(jnp.concatenate([t, p], axis=0).astype(BF16), _block_diag(p, lo_lanes)) for t, p in zip(ts, ps)]
        ts = [t + y[:CHUNK] for t, y in zip(ts, ys)]
        ps = [y[CHUNK:] for y in ys]
    return [t + _dot(t.astype(BF16), _block_diag(p, lo_lanes)) for t, p in zip(ts, ps)]


N_MIXER_PARAMS = 10


def _mixer_kernel(first, layer, streams, tl, mc, *refs):
    proj_ref = refs[0]
    convw_ref, gvec_ref, normg_ref, lng_ref, lnb_ref, ws_ref, bsb_ref, qg_ref, kg_ref, sink_ref = refs[1:1 + N_MIXER_PARAMS]
    rest = refs[1 + N_MIXER_PARAMS:]
    if first:
        mix_ref, s_out, conv_out, k_out, v_out = rest[:5]
    else:
        conv0_ref, s0_ref, k0_ref, v0_ref = rest[:4]
        mix_ref, s_out, conv_out, k_out, v_out, vrow_out = rest[4:10]
    s_scr, xbuf, kbuf, vbuf = rest[-4:]
    j = pl.program_id(1)
    nchunk = tl // CHUNK
    seqs = range(streams)

    @pl.when(j == 0)
    def _init():
        if first:
            s_scr[...] = jnp.zeros_like(s_scr)
            xbuf[:, 0:CONV_PAD, :] = jnp.zeros((streams, CONV_PAD, GDN_QKV), F32)
            kbuf[:, 0:WINDOW, :] = jnp.zeros((streams, WINDOW, SWA_KV), F32)
            vbuf[:, 0:WINDOW, :] = jnp.zeros((streams, WINDOW, SWA_KV), F32)
        else:
            for g in seqs:
                for h in range(GDN_HEADS):
                    s_scr[g, h * GDN_DK:(h + 1) * GDN_DK, :] = s0_ref[g, h]
            xbuf[:, 0:CONV_PAD, :] = conv0_ref[...]
            kbuf[:, 0:WINDOW, :] = k0_ref[...]
            vbuf[:, 0:WINDOW, :] = v0_ref[...]

    row_w = lax.broadcasted_iota(jnp.int32, (CHUNK, LANES), 0)
    lane_w = lax.broadcasted_iota(jnp.int32, (CHUNK, LANES), 1)
    lane_in = lane_w & (CHUNK - 1)
    lo_w = lane_w < CHUNK
    incl_w = lane_in <= row_w
    strict_w = lane_in < row_w
    eye_w = jnp.where(lane_in == row_w, 1.0, 0.0).astype(F32)
    top_st = lax.broadcasted_iota(jnp.int32, (2 * CHUNK, LANES), 0) < CHUNK
    head0_rows = (lax.broadcasted_iota(jnp.int32, (4 * CHUNK, LANES), 0) & CHUNK) == 0
    zero_blk = jnp.zeros((CHUNK, GDN_DK), F32)
    head_rows = jnp.where(lax.broadcasted_iota(jnp.int32, (8, LANES), 1)
                          == lax.broadcasted_iota(jnp.int32, (8, LANES), 0) + GDN_HEADS, 1.0, 0.0).astype(BF16)
    lo_row = lax.broadcasted_iota(jnp.int32, (1, LANES), 1) < CHUNK
    r_i = lax.broadcasted_iota(jnp.int32, (tl, tl), 0)
    c_i = lax.broadcasted_iota(jnp.int32, (tl, tl), 1)
    cum_mat = jnp.where((r_i // CHUNK == c_i // CHUNK) & (c_i <= r_i), 1.0, 0.0).astype(BF16)
    spread_beta = _lane_spread_matrix(0, GDN_HEADS)
    spread_gc = _lane_spread_matrix(GDN_HEADS, GDN_HEADS)
    mean_mlp = _group_mean_matrix(MLP_WIDTH, MLP_HEAD_DIM)

    qn, kn, vv, beta_b, gc_b, gc_all, gc_row_blk, conv_tail, u_act, v_norm, qn_swa = ({} for _ in range(11))
    for g in seqs:
        xbuf[g, CONV_PAD:CONV_PAD + tl, :] = proj_ref[g, :, C_QKV:C_QKV + GDN_QKV]

        def conv_silu(col):
            cols = slice(col, col + LANES)
            x = xbuf[g, :, cols]
            x1 = pltpu.roll(x, 1, axis=0)
            near = x * convw_ref[3:4, cols] + x1 * convw_ref[2:3, cols]
            far = x * convw_ref[1:2, cols] + x1 * convw_ref[0:1, cols]
            return _silu((near + pltpu.roll(far, 2, axis=0))[CONV_PAD:])

        for h in range(GDN_HEADS):
            q_h = conv_silu(h * GDN_DK)
            k_h = conv_silu(GDN_OUT + h * GDN_DK)
            qn[g, h] = q_h * lax.rsqrt(jnp.sum(q_h * q_h, axis=-1, keepdims=True) + EPS) * (GDN_DK ** -0.5)
            kn[g, h] = k_h * lax.rsqrt(jnp.sum(k_h * k_h, axis=-1, keepdims=True) + EPS)
            vv[g, h] = conv_silu(2 * GDN_OUT + h * GDN_DV)
        conv_tail[g] = xbuf[g, tl:tl + CONV_PAD, :]
        xbuf[g, 0:CONV_PAD, :] = conv_tail[g]

        ba = proj_ref[g, :, C_BA:C_BA + LANES]
        beta_all = _sigmoid(ba)
        g_all = -jnp.exp(gvec_ref[0:1, :]) * _softplus(ba + gvec_ref[1:2, :])
        g_hi, g_mid, g_lo = _split3(g_all)
        gc_all[g] = _dot(cum_mat, g_hi) + _dot(cum_mat, g_mid) + _dot(cum_mat, g_lo)
        gc_parts = _split3(gc_all[g] if tl >= LANES else jnp.concatenate([gc_all[g]] * (LANES // tl), axis=0))
        gc_rows = (_dot_nt(head_rows, gc_parts[0]) + _dot_nt(head_rows, gc_parts[1])
                   + _dot_nt(head_rows, gc_parts[2]))
        for k in range(gc_rows.shape[1] // LANES):
            blk = gc_rows[:, k * LANES:(k + 1) * LANES]
            gc_row_blk[g, k] = (blk, pltpu.roll(blk, CHUNK, axis=1))
        beta_b[g] = _dot_split3(beta_all, spread_beta)
        gc_b[g] = _dot_split3(gc_all[g], spread_gc)

        u_act[g] = jax.nn.gelu(proj_ref[g, :, C_U:C_U + MLP_WIDTH])
        v_act = jax.nn.gelu(proj_ref[g, :, C_VM:C_VM + MLP_WIDTH])
        v_cent = v_act - _dot_split(v_act, mean_mlp)
        v_var = _dot_split(v_cent * v_cent, mean_mlp)
        v_norm[g] = v_cent * lax.rsqrt(v_var + EPS) * lng_ref[...] + lnb_ref[...]
        if not first:
            vrow_out[g] = v_norm[g]

        sq = proj_ref[g, :, C_SQ:C_SQ + SWA_Q]
        sk = proj_ref[g, :, C_SK:C_SK + SWA_KV]
        qn_swa[g] = sq * lax.rsqrt(_dot_split(sq * sq, mean_mlp) + EPS) * qg_ref[...]
        kbuf[g, WINDOW:WINDOW + tl, :] = (sk * lax.rsqrt(_dot_split(sk * sk, mean_mlp[:SWA_KV, :SWA_KV]) + EPS)
                                          * kg_ref[...])
        vbuf[g, WINDOW:WINDOW + tl, :] = proj_ref[g, :, C_SV:C_SV + SWA_KV]

    items = []
    for c in range(nchunk):
        rows = slice(c * CHUNK, (c + 1) * CHUNK)
        for g in seqs:
            blk, blk_swapped = gc_row_blk[g, c * CHUNK // LANES]
            for p in range(GDN_PAIRS):
                h0, h1 = 2 * p, 2 * p + 1
                q0, q1, k0, k1 = qn[g, h0][rows], qn[g, h1][rows], kn[g, h0][rows], kn[g, h1][rows]
                v0, v1 = vv[g, h0][rows], vv[g, h1][rows]
                b0, b1 = beta_b[g][rows, h0 * LANES:(h0 + 1) * LANES], beta_b[g][rows, h1 * LANES:(h1 + 1) * LANES]
                g0, g1 = gc_b[g][rows, h0 * LANES:(h0 + 1) * LANES], gc_b[g][rows, h1 * LANES:(h1 + 1) * LANES]
                kb0, kb1 = k0 * b0, k1 * b1
                lhs = jnp.concatenate([jnp.concatenate([kb0, kb1], axis=-1),
                                       jnp.concatenate([q0, q1], axis=-1)], axis=0).astype(BF16)
                rhs = jnp.concatenate([jnp.concatenate([k0, zero_blk], axis=-1),
                                       jnp.concatenate([zero_blk, k1], axis=-1)], axis=0).astype(BF16)
                x = _dot_nt(lhs, rhs)
                if (c * CHUNK) % LANES == 0:
                    grow_w = jnp.where(lo_row, blk[h0:h0 + 1], blk_swapped[h1:h1 + 1])
                else:
                    grow_w = jnp.where(lo_row, blk_swapped[h0:h0 + 1], blk[h1:h1 + 1])
                decay = jnp.exp(jnp.where(incl_w, jnp.where(lo_w, g0, g1) - grow_w, -jnp.inf))
                e0, e1 = jnp.exp(g0), jnp.exp(g1)
                gl0, gl1 = g0[CHUNK - 1:CHUNK, :], g1[CHUNK - 1:CHUNK, :]
                items.append(dict(
                    c=c, g=g, p=p,
                    a=jnp.where(strict_w, x[:CHUNK] * decay, 0.0),
                    attn=x[CHUNK:] * decay,
                    rhs=jnp.concatenate([jnp.concatenate([v0 * b0, kb0 * e0], axis=-1),
                                         jnp.concatenate([v1 * b1, kb1 * e1], axis=-1)], axis=0).astype(BF16),
                    qg=jnp.concatenate([q0 * e0, q1 * e1], axis=0),
                    kd=jnp.concatenate([k0 * jnp.exp(gl0 - g0), k1 * jnp.exp(gl1 - g1)], axis=0),
                    scale=jnp.concatenate([jnp.broadcast_to(jnp.exp(gl0), (GDN_DK, GDN_DV)),
                                           jnp.broadcast_to(jnp.exp(gl1), (GDN_DK, GDN_DV))], axis=0)))
    tinvs = _unit_lower_inverse_wide([it["a"] for it in items], eye_w, lo_w)
    for it, tinv in zip(items, tinvs):
        it["sol"] = _dot(_block_diag(tinv, lo_w), it["rhs"])

    pair_rows = [slice(p * 2 * GDN_DK, (p + 1) * 2 * GDN_DK) for p in range(GDN_PAIRS)]
    states = {(g, p): s_scr[g, pair_rows[p], :] for g in seqs for p in range(GDN_PAIRS)}

    def gdn_scan_read(it):
        wq = jnp.concatenate([it["sol"][:, GDN_DV:], it["qg"]], axis=0)
        lhs = jnp.concatenate([jnp.where(head0_rows, wq, 0.0), jnp.where(head0_rows, 0.0, wq)], axis=-1)
        return _dot(lhs.astype(BF16), states[it["g"], it["p"]].astype(BF16))

    def gdn_scan_update(it, y):
        g, p, rows = it["g"], it["p"], slice(it["c"] * CHUNK, (it["c"] + 1) * CHUNK)
        v_new = (it["sol"][:, :GDN_DV] - y[:2 * CHUNK]).astype(BF16)
        o = y[2 * CHUNK:] + _dot(_block_diag(it["attn"], lo_w), v_new)
        kd_w = jnp.concatenate([jnp.where(top_st, it["kd"], 0.0), jnp.where(top_st, 0.0, it["kd"])], axis=-1)
        states[g, p] = states[g, p] * it["scale"] + _dot_tn(kd_w.astype(BF16), v_new)
        o = o * lax.rsqrt(jnp.mean(o * o, axis=-1, keepdims=True) + EPS) * normg_ref[...]
        for i in range(2):
            h = 2 * p + i
            z = proj_ref[g, rows, C_Z + h * GDN_DV:C_Z + (h + 1) * GDN_DV]
            mix_ref[g, rows, h * GDN_DV:(h + 1) * GDN_DV] = (o[i * CHUNK:(i + 1) * CHUNK] * _silu(z)).astype(mix_ref.dtype)

    mi = lax.broadcasted_iota(jnp.int32, (mc, mc), 0)
    mj = lax.broadcasted_iota(jnp.int32, (mc, mc), 1)
    block_causal = (mi // CHUNK) >= (mj // CHUNK)
    low_half = lax.broadcasted_iota(jnp.int32, (mc, LANES), 1) < MLP_HEAD_DIM
    ws_causal = [jnp.where(block_causal, ws_ref[h, :mc, :mc], 0.0).astype(BF16) for h in range(MLP_HEADS)]

    def gmlp_chunk(g, m):
        rows = slice(m * mc, (m + 1) * mc)
        for p in range(MLP_WIDTH // LANES):
            lanes = slice(p * LANES, (p + 1) * LANES)
            vpair = v_norm[g][rows, lanes].astype(BF16)
            s = jnp.where(low_half, _dot(ws_causal[2 * p], vpair), _dot(ws_causal[2 * p + 1], vpair)) + bsb_ref[:mc, lanes]
            mix_ref[g, rows, GDN_OUT + p * LANES:GDN_OUT + (p + 1) * LANES] = (u_act[g][rows, lanes] * s).astype(mix_ref.dtype)

    nkeys = WINDOW + CHUNK
    top_rows = lax.broadcasted_iota(jnp.int32, (2 * CHUNK, 1), 0) < CHUNK
    key_off = lax.broadcasted_iota(jnp.int32, (2 * CHUNK, nkeys), 1)

    def swa_scores(g, c):
        rows = slice(c * CHUNK, (c + 1) * CHUNK)
        keys = kbuf[g, c * CHUNK:c * CHUNK + nkeys, :].astype(BF16)
        out = []
        for grp in range(SWA_GROUP):
            qg = qn_swa[g][rows, grp * LANES:(grp + 1) * LANES]
            qstack = jnp.concatenate([jnp.where(lo_w, qg, 0.0), jnp.where(lo_w, 0.0, qg)], axis=0)
            out.append(_dot_nt(qstack.astype(BF16), keys))
        return out

    def swa_attend(g, c, scores):
        rows = slice(c * CHUNK, (c + 1) * CHUNK)
        vals = vbuf[g, c * CHUNK:c * CHUNK + nkeys, :].astype(BF16)
        for grp in range(SWA_GROUP):
            s = scores[grp] * SWA_SCALE
            if first:
                key_pos = j * tl + (c * CHUNK - WINDOW) + key_off
                s = jnp.where(key_pos >= 0, s, -jnp.inf)
            sink = jnp.where(top_rows, sink_ref[layer, grp], sink_ref[layer, SWA_GROUP + grp])
            mx = jnp.maximum(jnp.max(s, axis=-1, keepdims=True), sink)
            e = jnp.exp(s - mx)
            inv_den = 1.0 / (jnp.sum(e, axis=-1, keepdims=True) + jnp.exp(sink - mx))
            pv = _dot((e * inv_den).astype(BF16), vals)
            o = jnp.where(lo_w, pv[:CHUNK], pv[CHUNK:])
            col = GDN_OUT + MLP_WIDTH + grp * LANES
            mix_ref[g, rows, col:col + LANES] = o.astype(mix_ref.dtype)

    per_chunk = streams * GDN_PAIRS
    for c in range(nchunk):
        chunk_items = items[c * per_chunk:(c + 1) * per_chunk]
        ys = [gdn_scan_read(it) for it in chunk_items]
        scores = [swa_scores(g, c) for g in seqs]
        for it, y in zip(chunk_items, ys):
            gdn_scan_update(it, y)
        for g in seqs:
            swa_attend(g, c, scores[g])
            if ((c + 1) * CHUNK) % mc == 0:
                gmlp_chunk(g, ((c + 1) * CHUNK) // mc - 1)
    for (g, p), state in states.items():
        s_scr[g, pair_rows[p], :] = state

    k_tail = kbuf[:, tl:tl + WINDOW, :]
    v_tail = vbuf[:, tl:tl + WINDOW, :]
    if tl >= WINDOW:
        kbuf[:, 0:WINDOW, :] = k_tail
        vbuf[:, 0:WINDOW, :] = v_tail

    @pl.when(j == pl.num_programs(1) - 1)
    def _final():
        for g in seqs:
            for h in range(GDN_HEADS):
                s_out[g, h] = s_scr[g, h * GDN_DK:(h + 1) * GDN_DK, :]
            conv_out[g] = conv_tail[g]
        k_out[...] = k_tail
        v_out[...] = v_tail


def _mixers(first, proj3d, sw, layer, init, tl, streams):
    b, l, _ = proj3d.shape
    mc = MLP_CHUNK if first else CHUNK
    per_b = lambda shape: pl.BlockSpec((streams,) + shape, lambda bi, ji: (bi,) + (0,) * len(shape))
    tile = lambda width: pl.BlockSpec((streams, tl, width), lambda bi, ji: (bi, ji, 0))
    state_specs = [per_b((GDN_HEADS, GDN_DK, GDN_DV)), per_b((CONV_PAD, GDN_QKV)),
                   per_b((WINDOW, SWA_KV)), per_b((WINDOW, SWA_KV))]
    names = ("conv_w", "gvec", "gdn_norm_g", "ln_g", "ln_b", "ws", "bs_tile", "q_g", "k_g")
    in_specs = ([tile(PROJ_COLS)] + [_layer_block(layer, sw[n].shape[1:]) for n in names]
                + [pl.BlockSpec(memory_space=pltpu.SMEM)])
    args = [proj3d] + [sw[n] for n in names] + [sw["sinks"]]
    out_specs = [tile(MIX_WIDTH)] + state_specs
    out_shape = [jax.ShapeDtypeStruct((b, l, MIX_WIDTH), BF16),
                 jax.ShapeDtypeStruct((b, GDN_HEADS, GDN_DK, GDN_DV), F32),
                 jax.ShapeDtypeStruct((b, CONV_PAD, GDN_QKV), F32),
                 jax.ShapeDtypeStruct((b, WINDOW, SWA_KV), F32),
                 jax.ShapeDtypeStruct((b, WINDOW, SWA_KV), F32)]
    if not first:
        conv0, s0, k0, v0 = init
        in_specs += [state_specs[1], state_specs[0], state_specs[2], state_specs[3]]
        args += [conv0, s0, k0, v0]
        out_specs.append(tile(MLP_WIDTH))
        out_shape.append(jax.ShapeDtypeStruct((b, l, MLP_WIDTH), F32))
    return pl.pallas_call(
        functools.partial(_mixer_kernel, first, layer, streams, tl, mc),
        grid=(b // streams, l // tl),
        in_specs=in_specs,
        out_specs=out_specs,
        out_shape=out_shape,
        scratch_shapes=[pltpu.VMEM((streams, GDN_HEADS * GDN_DK, GDN_DV), F32),
                        pltpu.VMEM((streams, CONV_PAD + tl, GDN_QKV), F32),
                        pltpu.VMEM((streams, WINDOW + tl, SWA_KV), F32),
                        pltpu.VMEM((streams, WINDOW + tl, SWA_KV), F32)],
        compiler_params=pltpu.CompilerParams(dimension_semantics=("arbitrary", "arbitrary"),
                                             vmem_limit_bytes=VMEM_LIMIT),
        name="mixers_prompt" if first else "mixers_sample",
    )(*args)


def _stacked_weights(w_in, conv_w, a_log, dt_bias, gdn_norm_g, ln_g, ln_b, ws, bs, q_g, k_g, sinks,
                     w_out, norm1_g, norm2_g, w_gate, w_up, w_down):
    depth, d = w_in.shape[:2]
    offs = np.cumsum((0,) + IN_SIZES)
    qkv, z, b_raw, a_raw, u, vm, sq, sk, sv = [w_in[:, :, offs[i]:offs[i + 1]] for i in range(len(IN_SIZES))]
    order = np.asarray(SWA_HEAD_ORDER)
    sq = sq.reshape(depth, d, SWA_HEADS, SWA_HEAD_DIM)[:, :, order].reshape(depth, d, SWA_Q)
    ba = jnp.concatenate([b_raw, a_raw, jnp.zeros((depth, d, LANES - 2 * GDN_HEADS), F32)], axis=2)
    wo_swa = (w_out[:, GDN_OUT + MLP_WIDTH:].reshape(depth, SWA_HEADS, SWA_HEAD_DIM, d)[:, order]
              .reshape(depth, SWA_Q, d))
    lane_vec = lambda v: jnp.pad(v, ((0, 0), (GDN_HEADS, LANES - 2 * GDN_HEADS)))
    return dict(
        w_in=jnp.concatenate([qkv, z, u, vm, sq, sk, sv, ba], axis=2).astype(BF16),
        w_out=jnp.concatenate([w_out[:, :GDN_OUT + MLP_WIDTH], wo_swa], axis=1).astype(BF16),
        norm1_g=norm1_g[:, None, :], norm2_g=norm2_g[:, None, :],
        conv_w=conv_w,
        gvec=jnp.stack([lane_vec(a_log), lane_vec(dt_bias)], axis=1),
        gdn_norm_g=gdn_norm_g[:, None, :],
        ln_g=ln_g.reshape(depth, 1, MLP_WIDTH), ln_b=ln_b.reshape(depth, 1, MLP_WIDTH),
        ws=ws,
        bs_tile=jnp.repeat(jnp.swapaxes(bs, 1, 2), MLP_HEAD_DIM, axis=2),
        q_g=jnp.tile(q_g, (1, SWA_HEADS))[:, None, :], k_g=jnp.tile(k_g, (1, SWA_KV_HEADS))[:, None, :],
        sinks=sinks,
        w_gate=w_gate.astype(BF16), w_up=w_up.astype(BF16), w_down=w_down.astype(BF16),
    )


def _layer(x, first, sw, layer, init, tl, streams):
    b, l, d = x.shape
    x2d = x.reshape(b * l, d)
    proj = _in_proj(x2d, sw, layer, DENSE_TILE).reshape(b, l, PROJ_COLS)
    outs = _mixers(first, proj, sw, layer, init, tl, streams)
    y = _out_ffn(outs[0].reshape(b * l, MIX_WIDTH), x2d, sw, layer, DENSE_TILE)
    return y.reshape(b, l, d), outs[1:]


def kernel(x_prompt, x_sample, cache_swa_k, cache_swa_v, state_gdn, state_gdn_conv, norm1_g, w_in, gdn_conv_w, gdn_a_log, gdn_dt_bias, gdn_norm_g, mlp_ln_g, mlp_ln_b, mlp_ws, mlp_bs, swa_q_norm_g, swa_k_norm_g, swa_sinks, w_out, norm2_g, ffn_w_gate, ffn_w_up, ffn_w_down):
    depth = w_in.shape[0]
    bs_dec, dec_seq = x_sample.shape[:2]
    bp = x_prompt.shape[0]
    yp, ys = x_prompt, x_sample
    p_k, p_v, p_s, p_c = [], [], [], []
    s_k, s_v, s_s, s_c, s_m = [], [], [], [], []
    kv_shape = (WINDOW, SWA_KV_HEADS, SWA_HEAD_DIM)
    conv_rows = slice(CONV_PAD - (GDN_CONV - 1), CONV_PAD)
    sw = _stacked_weights(w_in, gdn_conv_w, gdn_a_log, gdn_dt_bias, gdn_norm_g, mlp_ln_g, mlp_ln_b,
                          mlp_ws, mlp_bs, swa_q_norm_g, swa_k_norm_g, swa_sinks, w_out, norm1_g, norm2_g,
                          ffn_w_gate, ffn_w_up, ffn_w_down)
    conv0_all = jnp.pad(state_gdn_conv, ((0, 0), (0, 0), (CONV_PAD - (GDN_CONV - 1), 0), (0, 0)))
    for l in range(depth):
        yp, (st, cv, k_, v_) = _layer(yp, True, sw, l, None, PROMPT_TILE, PROMPT_STREAMS)
        p_k.append(k_.reshape((bp,) + kv_shape))
        p_v.append(v_.reshape((bp,) + kv_shape))
        p_s.append(st)
        p_c.append(cv[:, conv_rows])
        init = (conv0_all[l], state_gdn[l], cache_swa_k[l].reshape(bs_dec, WINDOW, SWA_KV),
                cache_swa_v[l].reshape(bs_dec, WINDOW, SWA_KV))
        ys, (st, cv, k_, v_, vr) = _layer(ys, False, sw, l, init, dec_seq, SAMPLE_STREAMS)
        s_k.append(k_.reshape((bs_dec,) + kv_shape))
        s_v.append(v_.reshape((bs_dec,) + kv_shape))
        s_s.append(st)
        s_c.append(cv[:, conv_rows])
        s_m.append(vr.reshape(bs_dec, dec_seq, MLP_HEADS, MLP_HEAD_DIM))
    return (yp, ys, jnp.stack(p_k), jnp.stack(p_v), jnp.stack(p_s), jnp.stack(p_c),
            jnp.stack(s_k), jnp.stack(s_v), jnp.stack(s_s), jnp.stack(s_c), jnp.stack(s_m))
```
